```python
import math
import jax, jax.numpy as jnp
from jax import lax
import numpy as np

D_MODEL = 2048
BATCH = 4
SEQ = 8192
DEPTH = 1

GDN_HEADS = 8
GDN_HEAD_DIM = 128
GDN_CONV = 4
GDN_CHUNK = 64
GDN_QK = GDN_HEADS * GDN_HEAD_DIM
GDN_V = GDN_HEADS * GDN_HEAD_DIM
GDN_CONV_CH = 2 * GDN_QK + GDN_V
MLA_HEADS = 8
QK_NOPE = 128
QK_ROPE = 64
V_HEAD = 128
Q_LORA = 512
KV_LORA = 512
ROPE_THETA = 10000.0
Q_BLOCK = 128
MAX_POS_OFFSET = 1024
MIX_WIDTH = GDN_V + MLA_HEADS * V_HEAD
D_FF = ((8 * D_MODEL + 3 * 256 - 1) // (3 * 256)) * 256
EPS = 1e-6
IN_WIDTH = GDN_CONV_CH + GDN_V + 2 * GDN_HEADS + Q_LORA + KV_LORA + QK_ROPE
IN_SPLITS = (
    GDN_CONV_CH,
    GDN_CONV_CH + GDN_V,
    GDN_CONV_CH + GDN_V + GDN_HEADS,
    GDN_CONV_CH + GDN_V + 2 * GDN_HEADS,
    GDN_CONV_CH + GDN_V + 2 * GDN_HEADS + Q_LORA,
    GDN_CONV_CH + GDN_V + 2 * GDN_HEADS + Q_LORA + KV_LORA,
)

kernel_name = "hymba_gdn_mla_swiglu_layer"


def rms_norm(x, w):
    xf = x.astype(jnp.float32)
    y = xf * lax.rsqrt(jnp.mean(xf * xf, axis=-1, keepdims=True) + EPS)
    return (y * w.astype(jnp.float32)).astype(x.dtype)


def l2_normalize(x):
    return x * lax.rsqrt(jnp.sum(x * x, axis=-1, keepdims=True) + EPS)


def rotary(x, positions):
    half = x.shape[-1] // 2
    inv_freq = ROPE_THETA ** (-jnp.arange(half, dtype=jnp.float32) / half)
    ang = positions.astype(jnp.float32)[:, :, None, None] * inv_freq
    cos, sin = jnp.cos(ang), jnp.sin(ang)
    xf = x.astype(jnp.float32)
    x1, x2 = xf[..., :half], xf[..., half:]
    out = jnp.concatenate([x1 * cos - x2 * sin, x2 * cos + x1 * sin], axis=-1)
    return out.astype(x.dtype)


def causal_short_conv(u, w):
    k, c = w.shape
    out = lax.conv_general_dilated(
        u, w[:, None, :].astype(u.dtype), window_strides=(1,), padding=[(k - 1, 0)],
        dimension_numbers=("NWC", "WIO", "NWC"), feature_group_count=c)
    return jax.nn.silu(out)


def gated_delta_rule_chunked(q, k, v, g, beta):
    b, s, h, dk = q.shape
    dv = v.shape[-1]
    c = GDN_CHUNK
    n = s // c
    f32 = jnp.float32
    q = l2_normalize(q.astype(f32)) * (dk ** -0.5)
    k = l2_normalize(k.astype(f32))
    v = v.astype(f32)

    def chunks(t):
        return t.reshape(b, n, c, h, t.shape[-1]).transpose(0, 3, 1, 2, 4)

    q, k, v = chunks(q), chunks(k), chunks(v)
    g = g.astype(f32).reshape(b, n, c, h).transpose(0, 3, 1, 2)
    beta = beta.astype(f32).reshape(b, n, c, h).transpose(0, 3, 1, 2)
    gc = jnp.cumsum(g, axis=-1)

    idx = jnp.arange(c)
    tril = idx[:, None] >= idx[None, :]
    strict = idx[:, None] > idx[None, :]
    decay = jnp.exp(jnp.where(tril, gc[..., :, None] - gc[..., None, :], -jnp.inf))

    kb = k * beta[..., None]
    vb = v * beta[..., None]
    a_mat = jnp.where(strict, jnp.einsum("bhncd,bhnjd->bhncj", kb, k) * decay, 0.0)
    lhs = a_mat + jnp.eye(c, dtype=f32)
    rhs = jnp.concatenate([vb, kb * jnp.exp(gc)[..., None]], axis=-1)
    sol = lax.linalg.triangular_solve(lhs, rhs, left_side=True, lower=True, unit_diagonal=True)
    u, w = sol[..., :dv], sol[..., dv:]
    intra = jnp.einsum("bhncd,bhnjd->bhncj", q, k) * decay

    def step(state, inp):
        q_c, k_c, u_c, w_c, gc_c, intra_c = inp
        v_new = u_c - jnp.einsum("bhcd,bhdv->bhcv", w_c, state)
        o = (jnp.einsum("bhcd,bhdv->bhcv", q_c * jnp.exp(gc_c)[..., None], state)
             + jnp.einsum("bhcj,bhjv->bhcv", intra_c, v_new))
        g_last = gc_c[..., -1]
        k_dec = k_c * jnp.exp(g_last[..., None] - gc_c)[..., None]
        state = state * jnp.exp(g_last)[..., None, None] + jnp.einsum("bhcd,bhcv->bhdv", k_dec, v_new)
        return state, o

    xs = tuple(jnp.moveaxis(t, 2, 0) for t in (q, k, u, w, gc, intra))
    state0 = jnp.zeros((b, h, dk, dv), f32)
    _, o = lax.scan(step, state0, xs)
    return o.transpose(1, 0, 3, 2, 4).reshape(b, s, h, dv)


def blocked_causal_mla_attention(q_nope, q_rope, k_nope, k_rope, v):
    b, s, h, dn = q_nope.shape
    nb = s // Q_BLOCK
    scale = (QK_NOPE + QK_ROPE) ** -0.5
    qn = q_nope.reshape(b, nb, Q_BLOCK, h, dn).swapaxes(0, 1)
    qr = q_rope.reshape(b, nb, Q_BLOCK, h, QK_ROPE).swapaxes(0, 1)
    key_pos = jnp.arange(s)

    def block(args):
        qn_b, qr_b, start = args
        sc = (jnp.einsum("bqhd,bkhd->bhqk", qn_b, k_nope)
              + jnp.einsum("bqhr,bkr->bhqk", qr_b, k_rope)).astype(jnp.float32) * scale
        q_pos = start + jnp.arange(Q_BLOCK)
        sc = jnp.where(key_pos[None, :] <= q_pos[:, None], sc, -jnp.inf)
        p = jax.nn.softmax(sc, axis=-1).astype(v.dtype)
        return jnp.einsum("bhqk,bkhd->bqhd", p, v)

    out = lax.map(block, (qn, qr, jnp.arange(nb) * Q_BLOCK))
    return out.swapaxes(0, 1).reshape(b, s, h, v.shape[-1])


def setup_inputs(seed: int = 0) -> dict:
    key = jax.random.key(seed)
    ks = jax.random.split(key, 20)
    L = DEPTH
    f32 = jnp.float32

    def nrm(k, shape, fan_in):
        return jax.random.normal(k, shape, f32) * fan_in ** -0.5

    def gain(k, shape):
        return 1.0 + 0.02 * jax.random.normal(k, shape, f32)

    x = jax.random.normal(ks[0], (BATCH, SEQ, D_MODEL), f32)
    offsets = jax.random.randint(ks[1], (BATCH, 1), 0, MAX_POS_OFFSET, dtype=jnp.int32)
    positions = offsets + jnp.arange(SEQ, dtype=jnp.int32)[None, :]
    attn_norm_w = gain(ks[2], (L, D_MODEL))
    w_in = nrm(ks[3], (L, D_MODEL, IN_WIDTH), D_MODEL)
    conv_w = nrm(ks[4], (L, GDN_CONV, GDN_CONV_CH), GDN_CONV)
    a_log = jnp.log(jax.random.uniform(ks[5], (L, GDN_HEADS), f32, 1.0, 16.0))
    dt = jnp.exp(jax.random.uniform(ks[6], (L, GDN_HEADS), f32, math.log(1e-3), math.log(1e-1)))
    dt_bias = dt + jnp.log(-jnp.expm1(-dt))
    gdn_norm_w = gain(ks[7], (L, GDN_HEAD_DIM))
    q_norm_w = gain(ks[8], (L, Q_LORA))
    w_uq = nrm(ks[9], (L, Q_LORA, MLA_HEADS * (QK_NOPE + QK_ROPE)), Q_LORA)
    kv_norm_w = gain(ks[10], (L, KV_LORA))
    w_ukv = nrm(ks[11], (L, KV_LORA, MLA_HEADS * (QK_NOPE + V_HEAD)), KV_LORA)
    mla_out_norm_w = gain(ks[12], (L, V_HEAD))
    w_out = nrm(ks[13], (L, MIX_WIDTH, D_MODEL), MIX_WIDTH)
    ffn_norm_w = gain(ks[14], (L, D_MODEL))
    w_gate = nrm(ks[15], (L, D_MODEL, D_FF), D_MODEL)
    w_up = nrm(ks[16], (L, D_MODEL, D_FF), D_MODEL)
    w_down = nrm(ks[17], (L, D_FF, D_MODEL), D_FF)
    final_norm_w = gain(ks[18], (D_MODEL,))
    return {"x": x, "positions": positions, "attn_norm_w": attn_norm_w, "w_in": w_in,
            "conv_w": conv_w, "a_log": a_log, "dt_bias": dt_bias, "gdn_norm_w": gdn_norm_w,
            "q_norm_w": q_norm_w, "w_uq": w_uq, "kv_norm_w": kv_norm_w, "w_ukv": w_ukv,
            "mla_out_norm_w": mla_out_norm_w, "w_out": w_out, "ffn_norm_w": ffn_norm_w,
            "w_gate": w_gate, "w_up": w_up, "w_down": w_down, "final_norm_w": final_norm_w}


def reference(x, positions, attn_norm_w, w_in, conv_w, a_log, dt_bias, gdn_norm_w,
              q_norm_w, w_uq, kv_norm_w, w_ukv, mla_out_norm_w, w_out, ffn_norm_w,
              w_gate, w_up, w_down, final_norm_w):
    b, s, _ = x.shape
    for l in range(DEPTH):
        h = rms_norm(x, attn_norm_w[l])
        proj = h @ w_in[l]
        qkv_pre, z, b_raw, a_raw, cq, ckv, kr = jnp.split(proj, IN_SPLITS, axis=-1)

        qkv = causal_short_conv(qkv_pre, conv_w[l])
        gq, gk, gv = jnp.split(qkv, (GDN_QK, 2 * GDN_QK), axis=-1)
        gq = gq.reshape(b, s, GDN_HEADS, GDN_HEAD_DIM)
        gk = gk.reshape(b, s, GDN_HEADS, GDN_HEAD_DIM)
        gv = gv.reshape(b, s, GDN_HEADS, GDN_HEAD_DIM)
        beta = jax.nn.sigmoid(b_raw.astype(jnp.float32))
        g = -jnp.exp(a_log[l].astype(jnp.float32)) * jax.nn.softplus(
            a_raw.astype(jnp.float32) + dt_bias[l].astype(jnp.float32))
        o_gdn = gated_delta_rule_chunked(gq, gk, gv, g, beta).astype(x.dtype)
        o_gdn = rms_norm(o_gdn, gdn_norm_w[l]) * jax.nn.silu(z.reshape(b, s, GDN_HEADS, GDN_HEAD_DIM))

        q = (rms_norm(cq, q_norm_w[l]) @ w_uq[l]).reshape(b, s, MLA_HEADS, QK_NOPE + QK_ROPE)
        q_nope, q_rope = q[..., :QK_NOPE], rotary(q[..., QK_NOPE:], positions)
        kv = (rms_norm(ckv, kv_norm_w[l]) @ w_ukv[l]).reshape(b, s, MLA_HEADS, QK_NOPE + V_HEAD)
        k_nope, v = kv[..., :QK_NOPE], kv[..., QK_NOPE:]
        k_rope = rotary(kr[:, :, None, :], positions)[:, :, 0, :]
        o_mla = blocked_causal_mla_attention(q_nope, q_rope, k_nope, k_rope, v)
        o_mla = rms_norm(o_mla, mla_out_norm_w[l])

        mixed = jnp.concatenate([o_gdn.reshape(b, s, GDN_V), o_mla.reshape(b, s, MLA_HEADS * V_HEAD)], axis=-1)
        x = x + mixed @ w_out[l]

        h = rms_norm(x, ffn_norm_w[l])
        x = x + (jax.nn.silu(h @ w_gate[l]) * (h @ w_up[l])) @ w_down[l]
    return rms_norm(x, final_norm_w)
```

```python
import functools
import math

import jax
import jax.numpy as jnp
from jax import lax
from jax.experimental import pallas as pl
from jax.experimental.pallas import tpu as pltpu

F32 = jnp.float32
BF16 = jnp.bfloat16

GDN_HEADS = 8
HEAD_DIM = 128
GDN_CONV = 4
GDN_CHUNK = 64
MLA_HEADS = 8
QK_NOPE = 128
QK_ROPE = 64
V_HEAD = 128
Q_LORA = 512
KV_LORA = 512
ROPE_THETA = 10000.0
EPS = 1e-6

LANES = 128
SUBLANES = 8
VMEM_LIMIT_BYTES = 56 * 2**20

INPROJ_TM = 1024
INPROJ_TN = 768
GDN_T = 1024
GDN_BLK = 2 * GDN_CHUNK
MLA_TM = 512
ATT_TQ = 512
ATT_TK = 512
OUT_TM = 512
FFN_TM = 512
FFN_TF = 512

QK_PAD = 256
SOFTMAX_SCALE = (QK_NOPE + QK_ROPE) ** -0.5
LOG2E = math.log2(math.e)


def _cparams(semantics):
    return pltpu.CompilerParams(dimension_semantics=semantics, vmem_limit_bytes=VMEM_LIMIT_BYTES)


def _sigmoid(x):
    return 1.0 / (1.0 + jnp.exp(-x))


def _softplus(x):
    return jnp.maximum(x, 0.0) + jnp.log1p(jnp.exp(-jnp.abs(x)))


def _dot(a, b):
    return jnp.dot(a, b, preferred_element_type=F32)


def _dot_nt(a, b):
    return lax.dot_general(a, b, (((1,), (1,)), ((), ())), preferred_element_type=F32)


def _dot_tn(a, b):
    return lax.dot_general(a, b, (((0,), (0,)), ((), ())), preferred_element_type=F32)


def _rope_kernel(pos_ref, invf_ref, cos_ref, sin_ref):
    ang = pos_ref[...].astype(F32) * invf_ref[...]
    cos_ref[...] = jnp.cos(ang)
    sin_ref[...] = jnp.sin(ang)


def _rope_tables(positions):
    b, s = positions.shape
    m = b * s
    half = QK_ROPE // 2
    per_row = LANES // half
    rows = m // per_row
    blk = min(rows, 1024)
    inv_freq = ROPE_THETA ** (-jnp.arange(half, dtype=F32) / half)
    invf = jnp.tile(inv_freq, per_row).reshape(1, LANES)
    pos_rep = jnp.repeat(positions.reshape(m), half).reshape(rows, LANES)
    cos, sin = pl.pallas_call(
        _rope_kernel,
        grid=(rows // blk,),
        in_specs=[pl.BlockSpec((blk, LANES), lambda i: (i, 0)),
                  pl.BlockSpec((1, LANES), lambda i: (0, 0))],
        out_specs=[pl.BlockSpec((blk, LANES), lambda i: (i, 0)),
                   pl.BlockSpec((blk, LANES), lambda i: (i, 0))],
        out_shape=[jax.ShapeDtypeStruct((rows, LANES), F32)] * 2,
        compiler_params=_cparams(("parallel",)),
        name="rope_table",
    )(pos_rep, invf)
    c = cos.reshape(m, half)
    sn = sin.reshape(m, half)
    zero = jnp.zeros((m, LANES - QK_ROPE), F32)
    cos_tab = jnp.concatenate([c, c, zero], axis=1).reshape(b, s, LANES)
    sin_tab = jnp.concatenate([-sn, sn, zero], axis=1).reshape(b, s, LANES)
    return cos_tab, sin_tab


def _rotate_half_unsigned(x, lane):
    half = QK_ROPE // 2
    return jnp.where(lane < half, pltpu.roll(x, LANES - half, axis=1), pltpu.roll(x, half, axis=1))


def _inproj_kernel(x_ref, nw_ref, w_ref, o_ref, small_ref, h_ref, *, nj, small_off):
    j = pl.program_id(1)

    @pl.when(j == 0)
    def _():
        x = x_ref[...]
        ms = jnp.mean(x * x, axis=-1, keepdims=True)
        h_ref[...] = (x * lax.rsqrt(ms + EPS) * nw_ref[...]).astype(BF16)

    acc = _dot(h_ref[...], w_ref[...])
    o_ref[...] = acc.astype(BF16)

    @pl.when(j == nj - 1)
    def _():
        small_ref[...] = acc[:, small_off:small_off + LANES]


def _in_projection(x2, norm_w, w_in):
    m, d = x2.shape
    conv_ch = 3 * GDN_HEADS * HEAD_DIM
    gdn_v = GDN_HEADS * HEAD_DIM
    o = 0
    qkv = w_in[:, o:o + conv_ch]; o += conv_ch
    z = w_in[:, o:o + gdn_v]; o += gdn_v
    b_raw = w_in[:, o:o + GDN_HEADS]; o += GDN_HEADS
    a_raw = w_in[:, o:o + GDN_HEADS]; o += GDN_HEADS
    cq = w_in[:, o:o + Q_LORA]; o += Q_LORA
    ckv = w_in[:, o:o + KV_LORA]; o += KV_LORA
    kr = w_in[:, o:o + QK_ROPE]
    main_w = conv_ch + gdn_v + Q_LORA + KV_LORA
    small_used = QK_ROPE + 2 * GDN_HEADS
    tn = INPROJ_TN
    total = pl.cdiv(main_w + LANES, tn) * tn
    pad = jnp.zeros((d, total - main_w - small_used), w_in.dtype)
    w = jnp.concatenate([qkv, z, cq, ckv, kr, b_raw, a_raw, pad], axis=1).astype(BF16)
    nj = total // tn
    small_off = main_w - (nj - 1) * tn
    tm = INPROJ_TM
    kern = functools.partial(_inproj_kernel, nj=nj, small_off=small_off)
    proj, small = pl.pallas_call(
        kern,
        grid=(m // tm, nj),
        in_specs=[pl.BlockSpec((tm, d), lambda i, j: (i, 0)),
                  pl.BlockSpec((1, d), lambda i, j: (0, 0)),
                  pl.BlockSpec((d, tn), lambda i, j: (0, j))],
        out_specs=[pl.BlockSpec((tm, tn), lambda i, j: (i, j)),
                   pl.BlockSpec((tm, LANES), lambda i, j: (i, 0))],
        out_shape=[jax.ShapeDtypeStruct((m, total), BF16),
                   jax.ShapeDtypeStruct((m, LANES), F32)],
        scratch_shapes=[pltpu.VMEM((tm, d), BF16)],
        compiler_params=_cparams(("parallel", "arbitrary")),
        name="in_projection",
    )(x2, norm_w.reshape(1, d), w)
    return proj, small


def _gdn_kernel(alog_ref, dtb_ref, q_ref, k_ref, v_ref, z_ref, a_ref, b_ref,
                cwq_ref, cwk_ref, cwv_ref, nw_ref, o_ref,
                state_ref, xq_ref, xk_ref, xv_ref, qn_ref, kn_ref, vv_ref, gc_ref, beta_ref, oacc_ref):
    h = pl.program_id(1)
    t = pl.program_id(2)
    T = q_ref.shape[1]
    hist = SUBLANES
    blk = GDN_BLK
    c = GDN_CHUNK

    @pl.when(t == 0)
    def _():
        state_ref[...] = jnp.zeros_like(state_ref)
        zero_hist = jnp.zeros((hist, HEAD_DIM), F32)
        xq_ref[0:hist, :] = zero_hist
        xk_ref[0:hist, :] = zero_hist
        xv_ref[0:hist, :] = zero_hist

    def conv_silu(in_ref, xbuf, cw_ref):
        xbuf[hist:hist + T, :] = in_ref[0].astype(F32)
        w = cw_ref[...]
        y = xbuf[hist - 3:hist - 3 + T, :] * w[0:1, :]
        for i in range(1, GDN_CONV):
            y = y + xbuf[hist - 3 + i:hist - 3 + i + T, :] * w[i:i + 1, :]
        xbuf[0:hist, :] = xbuf[T:T + hist, :]
        return y * _sigmoid(y)

    q = conv_silu(q_ref, xq_ref, cwq_ref)
    k = conv_silu(k_ref, xk_ref, cwk_ref)
    vv_ref[...] = conv_silu(v_ref, xv_ref, cwv_ref)
    qn_ref[...] = q * lax.rsqrt(jnp.sum(q * q, axis=-1, keepdims=True) + EPS) * (HEAD_DIM ** -0.5)
    kn_ref[...] = k * lax.rsqrt(jnp.sum(k * k, axis=-1, keepdims=True) + EPS)

    a_raw = a_ref[0, 0]
    b_raw = b_ref[0, 0]
    beta_ref[...] = _sigmoid(b_raw)
    neg_a = -jnp.exp(jnp.full(a_raw.shape, alog_ref[h], F32))
    g = neg_a * _softplus(a_raw + dtb_ref[h])
    lane = lax.broadcasted_iota(jnp.int32, g.shape, 1)
    in_chunk = lane % c
    gc = g
    s = 1
    while s < c:
        gc = gc + jnp.where(in_chunk >= s, pltpu.roll(gc, s, axis=1), 0.0)
        s *= 2
    gc_ref[...] = gc

    row = lax.broadcasted_iota(jnp.int32, (blk, blk), 0)
    col = lax.broadcasted_iota(jnp.int32, (blk, blk), 1)
    same_chunk = (row // c) == (col // c)
    m_tril = jnp.logical_and(same_chunk, row >= col)
    m_strict = jnp.logical_and(same_chunk, row > col)
    eye = jnp.where(row == col, 1.0, 0.0).astype(F32)
    first = row < c

    def block(r, carry):
        rows = pl.ds(pl.multiple_of(r * blk, blk), blk)
        qn = qn_ref[rows, :]
        kn = kn_ref[rows, :]
        v = vv_ref[rows, :]
        g_row = jnp.broadcast_to(gc_ref[pl.ds(r, 1), :], (blk, blk))
        g_col = g_row.T
        b_col = jnp.broadcast_to(beta_ref[pl.ds(r, 1), :], (blk, blk)).T
        decay = jnp.exp(jnp.where(m_tril, g_col - g_row, -jnp.inf))
        kb = kn * b_col
        vb = v * b_col
        kn16 = kn.astype(BF16)
        a_mat = jnp.where(m_strict, _dot_nt(kb.astype(BF16), kn16) * decay, 0.0)
        intra = _dot_nt(qn.astype(BF16), kn16) * decay
        inv = eye - a_mat
        p = a_mat
        n = 2
        while n < c:
            p16 = p.astype(BF16)
            p = _dot(p16, p16)
            inv = inv + _dot(inv.astype(BF16), p.astype(BF16))
            n *= 2
        eg = jnp.exp(g_col)
        rhs = jnp.concatenate([vb, kb * eg], axis=1).astype(BF16)
        uw = _dot(inv.astype(BF16), rhs)
        u = uw[:, :HEAD_DIM]
        w16 = uw[:, HEAD_DIM:].astype(BF16)
        g_last1 = jnp.broadcast_to(g_col[c - 1:c, :], (blk, blk))
        g_last2 = jnp.broadcast_to(g_col[blk - 1:blk, :], (blk, blk))
        g_last = jnp.where(first, g_last1, g_last2)
        kdec16 = (kn * jnp.exp(g_last - g_col)).astype(BF16)
        qg16 = (qn * eg).astype(BF16)

        s0 = state_ref[...]
        s016 = s0.astype(BF16)
        vn1 = u[0:c] - _dot(w16[0:c], s016)
        oq1 = _dot(qg16[0:c], s016)
        s1 = s0 * jnp.exp(g_last1) + _dot_tn(kdec16[0:c], vn1.astype(BF16))
        s116 = s1.astype(BF16)
        vn2 = u[c:blk] - _dot(w16[c:blk], s116)
        oq2 = _dot(qg16[c:blk], s116)
        s2 = s1 * jnp.exp(g_last2) + _dot_tn(kdec16[c:blk], vn2.astype(BF16))
        state_ref[...] = s2
        vn = jnp.concatenate([vn1, vn2], axis=0).astype(BF16)
        oacc_ref[rows, :] = jnp.concatenate([oq1, oq2], axis=0) + _dot(intra.astype(BF16), vn)
        return carry

    lax.fori_loop(0, T // blk, block, 0)

    o = oacc_ref[...]
    zz = z_ref[0].astype(F32)
    y = o * lax.rsqrt(jnp.mean(o * o, axis=-1, keepdims=True) + EPS) * nw_ref[...]
    o_ref[0] = (y * (zz * _sigmoid(zz))).astype(o_ref.dtype)


def _gated_delta(proj3, small3, conv_w, a_log, dt_bias, norm_w):
    b, s, _ = proj3.shape
    hh = GDN_HEADS
    T = GDN_T
    tiles = T // LANES
    b_t = small3[:, :, QK_ROPE:QK_ROPE + hh].transpose(0, 2, 1).reshape(b, hh, s // LANES, LANES)
    a_t = small3[:, :, QK_ROPE + hh:QK_ROPE + 2 * hh].transpose(0, 2, 1).reshape(b, hh, s // LANES, LANES)
    col_spec = lambda off: pl.BlockSpec((1, T, HEAD_DIM), lambda bi, hi, ti: (bi, ti, off + hi))
    gate_spec = pl.BlockSpec((1, 1, tiles, LANES), lambda bi, hi, ti: (bi, hi, ti, 0))
    cw_spec = lambda off: pl.BlockSpec((GDN_CONV, HEAD_DIM), lambda bi, hi, ti: (0, off + hi))
    smem = pl.BlockSpec(memory_space=pltpu.SMEM)
    buf = pltpu.VMEM((T + SUBLANES, HEAD_DIM), F32)
    full = pltpu.VMEM((T, HEAD_DIM), F32)
    return pl.pallas_call(
        _gdn_kernel,
        grid=(b, hh, s // T),
        in_specs=[smem, smem,
                  col_spec(0), col_spec(hh), col_spec(2 * hh), col_spec(3 * hh),
                  gate_spec, gate_spec,
                  cw_spec(0), cw_spec(hh), cw_spec(2 * hh),
                  pl.BlockSpec((1, HEAD_DIM), lambda bi, hi, ti: (0, 0))],
        out_specs=pl.BlockSpec((1, T, HEAD_DIM), lambda bi, hi, ti: (bi, ti, hi)),
        out_shape=jax.ShapeDtypeStruct((b, s, hh * HEAD_DIM), BF16),
        scratch_shapes=[pltpu.VMEM((HEAD_DIM, HEAD_DIM), F32), buf, buf, buf, full, full, full,
                        pltpu.VMEM((tiles, LANES), F32), pltpu.VMEM((tiles, LANES), F32), full],
        compiler_params=_cparams(("parallel", "parallel", "arbitrary")),
        name="gated_delta",
    )(a_log, dt_bias, proj3, proj3, proj3, proj3, a_t, b_t, conv_w, conv_w, conv_w,
      norm_w.reshape(1, HEAD_DIM))


def _rms_rows(x, w):
    ms = jnp.mean(x * x, axis=-1, keepdims=True)
    return x * lax.rsqrt(ms + EPS) * w


def _qproj_kernel(c_ref, nw_ref, w_ref, cos_ref, sin_ref, o_ref):
    hq = _rms_rows(c_ref[0].astype(F32), nw_ref[...]).astype(BF16)
    res = _dot(hq, w_ref[...])
    cos = cos_ref[0]
    sin = sin_ref[0]
    lane = lax.broadcasted_iota(jnp.int32, cos.shape, 1)
    scale = SOFTMAX_SCALE * LOG2E
    for hd in range(MLA_HEADS):
        base = hd * QK_PAD
        nope = res[:, base:base + QK_NOPE]
        rp = res[:, base + QK_NOPE:base + QK_PAD]
        rot = rp * cos + _rotate_half_unsigned(rp, lane) * sin
        o_ref[0, hd, :, 0:QK_NOPE] = (nope * scale).astype(BF16)
        o_ref[0, hd, :, QK_NOPE:QK_PAD] = (rot * scale).astype(BF16)


def _kvproj_kernel(c_ref, small_ref, nw_ref, w_ref, cos_ref, sin_ref, k_ref, v_ref):
    hkv = _rms_rows(c_ref[0].astype(F32), nw_ref[...]).astype(BF16)
    res = _dot(hkv, w_ref[...])
    cos = cos_ref[0]
    sin = sin_ref[0]
    lane = lax.broadcasted_iota(jnp.int32, cos.shape, 1)
    kr = small_ref[0]
    kr16 = (kr * cos + _rotate_half_unsigned(kr, lane) * sin).astype(BF16)
    nk = MLA_HEADS * QK_NOPE
    for hd in range(MLA_HEADS):
        k_ref[0, hd, :, 0:QK_NOPE] = res[:, hd * QK_NOPE:(hd + 1) * QK_NOPE].astype(BF16)
        k_ref[0, hd, :, QK_NOPE:QK_PAD] = kr16
        v_ref[0, hd] = res[:, nk + hd * V_HEAD:nk + (hd + 1) * V_HEAD].astype(BF16)


def _mla_projections(proj3, small3, q_norm_w, w_uq, kv_norm_w, w_ukv, cos_tab, sin_tab):
    b, s, _ = proj3.shape
    hh = MLA_HEADS
    tm = MLA_TM
    main_cq = (3 * GDN_HEADS * HEAD_DIM + GDN_HEADS * HEAD_DIM) // Q_LORA
    main_ckv = main_cq + 1
    wq = w_uq.reshape(Q_LORA, hh, QK_NOPE + QK_ROPE)
    wq = jnp.concatenate([wq, jnp.zeros((Q_LORA, hh, QK_PAD - QK_NOPE - QK_ROPE), w_uq.dtype)], axis=2)
    wq = wq.reshape(Q_LORA, hh * QK_PAD).astype(BF16)
    wkv = w_ukv.reshape(KV_LORA, hh, QK_NOPE + V_HEAD)
    wkv = jnp.concatenate([wkv[:, :, :QK_NOPE].reshape(KV_LORA, hh * QK_NOPE),
                           wkv[:, :, QK_NOPE:].reshape(KV_LORA, hh * V_HEAD)], axis=1).astype(BF16)
    tab_spec = pl.BlockSpec((1, tm, LANES), lambda bi, ti: (bi, ti, 0))
    q = pl.pallas_call(
        _qproj_kernel,
        grid=(b, s // tm),
        in_specs=[pl.BlockSpec((1, tm, Q_LORA), lambda bi, ti: (bi, ti, main_cq)),
                  pl.BlockSpec((1, Q_LORA), lambda bi, ti: (0, 0)),
                  pl.BlockSpec((Q_LORA, hh * QK_PAD), lambda bi, ti: (0, 0)),
                  tab_spec, tab_spec],
        out_specs=pl.BlockSpec((1, hh, tm, QK_PAD), lambda bi, ti: (bi, 0, ti, 0)),
        out_shape=jax.ShapeDtypeStruct((b, hh, s, QK_PAD), BF16),
        compiler_params=_cparams(("parallel", "parallel")),
        name="mla_q_proj",
    )(proj3, q_norm_w.reshape(1, Q_LORA), wq, cos_tab, sin_tab)
    k, v = pl.pallas_call(
        _kvproj_kernel,
        grid=(b, s // tm),
        in_specs=[pl.BlockSpec((1, tm, KV_LORA), lambda bi, ti: (bi, ti, main_ckv)),
                  tab_spec,
                  pl.BlockSpec((1, KV_LORA), lambda bi, ti: (0, 0)),
                  pl.BlockSpec((KV_LORA, hh * (QK_NOPE + V_HEAD)), lambda bi, ti: (0, 0)),
                  tab_spec, tab_spec],
        out_specs=[pl.BlockSpec((1, hh, tm, QK_PAD), lambda bi, ti: (bi, 0, ti, 0)),
                   pl.BlockSpec((1, hh, tm, V_HEAD), lambda bi, ti: (bi, 0, ti, 0))],
        out_shape=[jax.ShapeDtypeStruct((b, hh, s, QK_PAD), BF16),
                   jax.ShapeDtypeStruct((b, hh, s, V_HEAD), BF16)],
        compiler_params=_cparams(("parallel", "parallel")),
        name="mla_kv_proj",
    )(proj3, small3, kv_norm_w.reshape(1, KV_LORA), wkv, cos_tab, sin_tab)
    return q, k, v


def _attn_kernel(q_ref, k_ref, v_ref, nw_ref, o_ref, m_ref, l_ref, acc_ref):
    i = pl.program_id(2)
    tq = q_ref.shape[2]
    tk = ATT_TK
    q = q_ref[0, 0]
    m_ref[...] = jnp.full(m_ref.shape, -jnp.inf, F32)
    l_ref[...] = jnp.zeros(l_ref.shape, F32)
    acc_ref[...] = jnp.zeros(acc_ref.shape, F32)

    def step(start, masked):
        kb = k_ref[0, 0, pl.ds(start, tk), :]
        vb = v_ref[0, 0, pl.ds(start, tk), :]
        sc = _dot_nt(q, kb)
        if masked:
            qpos = lax.broadcasted_iota(jnp.int32, sc.shape, 0)
            kpos = lax.broadcasted_iota(jnp.int32, sc.shape, 1)
            sc = jnp.where(kpos <= qpos, sc, -jnp.inf)
        m_prev = m_ref[...]
        m_new = jnp.maximum(m_prev, jnp.max(sc, axis=-1, keepdims=True))
        p = jnp.exp2(sc - m_new[:, 0:1])
        alpha = jnp.exp2(m_prev - m_new)
        l_ref[...] = alpha * l_ref[...] + jnp.sum(p, axis=-1, keepdims=True)
        acc_ref[...] = alpha * acc_ref[...] + _dot(p.astype(BF16), vb)
        m_ref[...] = m_new

    def body(j, carry):
        step(pl.multiple_of(j * tk, tk), False)
        return carry

    lax.fori_loop(0, i * (tq // tk), body, 0)
    step(pl.multiple_of(i * tq, tq), True)

    o = acc_ref[...] / l_ref[...]
    y = o * lax.rsqrt(jnp.mean(o * o, axis=-1, keepdims=True) + EPS) * nw_ref[...]
    o_ref[0] = y.astype(o_ref.dtype)


def _attention(q, k, v, norm_w):
    b, hh, s, _ = q.shape
    tq = ATT_TQ
    assert ATT_TQ == ATT_TK
    stat = pltpu.VMEM((tq, V_HEAD), F32)
    return pl.pallas_call(
        _attn_kernel,
        grid=(b, hh, s // tq),
        in_specs=[pl.BlockSpec((1, 1, tq, QK_PAD), lambda bi, hi, qi: (bi, hi, qi, 0)),
                  pl.BlockSpec((1, 1, s, QK_PAD), lambda bi, hi, qi: (bi, hi, 0, 0)),
                  pl.BlockSpec((1, 1, s, V_HEAD), lambda bi, hi, qi: (bi, hi, 0, 0)),
                  pl.BlockSpec((1, V_HEAD), lambda bi, hi, qi: (0, 0))],
        out_specs=pl.BlockSpec((1, tq, V_HEAD), lambda bi, hi, qi: (bi, qi, hi)),
        out_shape=jax.ShapeDtypeStruct((b, s, hh * V_HEAD), BF16),
        scratch_shapes=[stat, stat, stat],
        compiler_params=_cparams(("parallel", "parallel", "arbitrary")),
        name="mla_attention",
    )(q, k, v, norm_w.reshape(1, V_HEAD))


def _outproj_kernel(og_ref, om_ref, x_ref, wg_ref, wm_ref, o_ref):
    o_ref[...] = x_ref[...] + _dot(og_ref[...], wg_ref[...]) + _dot(om_ref[...], wm_ref[...])


def _out_projection(o_gdn, o_mla, x2, w_out):
    m, d = x2.shape
    kg = o_gdn.shape[1]
    km = o_mla.shape[1]
    tm = OUT_TM
    w16 = w_out.astype(BF16)
    return pl.pallas_call(
        _outproj_kernel,
        grid=(m // tm,),
        in_specs=[pl.BlockSpec((tm, kg), lambda i: (i, 0)),
                  pl.BlockSpec((tm, km), lambda i: (i, 0)),
                  pl.BlockSpec((tm, d), lambda i: (i, 0)),
                  pl.BlockSpec((kg, d), lambda i: (0, 0)),
                  pl.BlockSpec((km, d), lambda i: (0, 0))],
        out_specs=pl.BlockSpec((tm, d), lambda i: (i, 0)),
        out_shape=jax.ShapeDtypeStruct((m, d), F32),
        compiler_params=_cparams(("parallel",)),
        name="out_projection",
    )(o_gdn, o_mla, x2, w16[:kg], w16[kg:])


def _ffn_kernel(x_ref, nw_ref, wg_ref, wu_ref, wd_ref, fw_ref, o_ref, h_ref, *, nj):
    j = pl.program_id(1)

    @pl.when(j == 0)
    def _():
        h_ref[...] = _rms_rows(x_ref[...], nw_ref[...]).astype(BF16)

    h = h_ref[...]
    g = _dot(h, wg_ref[...])
    u = _dot(h, wu_ref[...])
    a = (g * _sigmoid(g) * u).astype(BF16)
    part = _dot(a, wd_ref[...])

    @pl.when(j == 0)
    def _():
        o_ref[...] = part

    @pl.when(j > 0)
    def _():
        o_ref[...] += part

    @pl.when(j == nj - 1)
    def _():
        o_ref[...] = _rms_rows(x_ref[...] + o_ref[...], fw_ref[...])


def _ffn(x1, norm_w, w_gate, w_up, w_down, final_w):
    m, d = x1.shape
    dff = w_gate.shape[1]
    tm = FFN_TM
    tf = FFN_TF
    nj = dff // tf
    kern = functools.partial(_ffn_kernel, nj=nj)
    return pl.pallas_call(
        kern,
        grid=(m // tm, nj),
        in_specs=[pl.BlockSpec((tm, d), lambda i, j: (i, 0)),
                  pl.BlockSpec((1, d), lambda i, j: (0, 0)),
                  pl.BlockSpec((d, tf), lambda i, j: (0, j)),
                  pl.BlockSpec((d, tf), lambda i, j: (0, j)),
                  pl.BlockSpec((tf, d), lambda i, j: (j, 0)),
                  pl.BlockSpec((1, d), lambda i, j: (0, 0))],
        out_specs=pl.BlockSpec((tm, d), lambda i, j: (i, 0)),
        out_shape=jax.ShapeDtypeStruct((m, d), F32),
        scratch_shapes=[pltpu.VMEM((tm, d), BF16)],
        compiler_params=_cparams(("parallel", "arbitrary")),
        name="swiglu_ffn",
    )(x1, norm_w.reshape(1, d), w_gate.astype(BF16), w_up.astype(BF16), w_down.astype(BF16),
      final_w.reshape(1, d))


def kernel(x, positions, attn_norm_w, w_in, conv_w, a_log, dt_bias, gdn_norm_w, q_norm_w, w_uq,
           kv_norm_w, w_ukv, mla_out_norm_w, w_out, ffn_norm_w, w_gate, w_up, w_down, final_norm_w):
    b, s, d = x.shape
    assert w_in.shape[0] == 1, "the final rmsnorm is fused into the (single) layer's FFN kernel"
    l = 0
    cos_tab, sin_tab = _rope_tables(positions)
    x2 = x.reshape(b * s, d)
    proj, small = _in_projection(x2, attn_norm_w[l], w_in[l])
    proj3 = proj.reshape(b, s, proj.shape[1])
    small3 = small.reshape(b, s, LANES)
    o_gdn = _gated_delta(proj3, small3, conv_w[l], a_log[l], dt_bias[l], gdn_norm_w[l])
    q, k, v = _mla_projections(proj3, small3, q_norm_w[l], w_uq[l], kv_norm_w[l], w_ukv[l],
                               cos_tab, sin_tab)
    o_mla = _attention(q, k, v, mla_out_norm_w[l])
    x1 = _out_projection(o_gdn.reshape(b * s, -1), o_mla.reshape(b * s, -1), x2, w_out[l])
    out = _ffn(x1, ffn_norm_w[l], w_gate[l], w_up[l], w_down[l], final_norm_w)
    return out.reshape(b, s, d)
```

```python
import functools
import math

import jax
import jax.numpy as jnp
from jax import lax
from jax.experimental import pallas as pl
from jax.experimental.pallas import tpu as pltpu

F32 = jnp.float32
BF16 = jnp.bfloat16

GDN_HEADS = 8
HEAD_DIM = 128
GDN_CONV = 4
GDN_CHUNK = 64
MLA_HEADS = 8
QK_NOPE = 128
QK_ROPE = 64
V_HEAD = 128
Q_LORA = 512
KV_LORA = 512
ROPE_THETA = 10000.0
EPS = 1e-6

LANES = 128
SUBLANES = 8
VMEM_LIMIT_BYTES = 56 * 2**20

INPROJ_TM = 1024
INPROJ_TN = 768
GDN_T = 512
GDN_BLK = 2 * GDN_CHUNK
MLA_TM = 512
ATT_TQ = 512
ATT_TK = 512
OUT_TM = 512
FFN_TM = 512
FFN_TF = 512

QK_PAD = 256
SOFTMAX_SCALE = (QK_NOPE + QK_ROPE) ** -0.5
LOG2E = math.log2(math.e)


def _cparams(semantics):
    return pltpu.CompilerParams(dimension_semantics=semantics, vmem_limit_bytes=VMEM_LIMIT_BYTES)


def _sigmoid(x):
    return 1.0 / (1.0 + jnp.exp(-x))


def _softplus(x):
    return jnp.maximum(x, 0.0) + jnp.log1p(jnp.exp(-jnp.abs(x)))


def _dot(a, b):
    return jnp.dot(a, b, preferred_element_type=F32)


def _dot_nt(a, b):
    return lax.dot_general(a, b, (((1,), (1,)), ((), ())), preferred_element_type=F32)


def _dot_tn(a, b):
    return lax.dot_general(a, b, (((0,), (0,)), ((), ())), preferred_element_type=F32)


def _rope_kernel(pos_ref, invf_ref, cos_ref, sin_ref):
    ang = pos_ref[...].astype(F32) * invf_ref[...]
    cos_ref[...] = jnp.cos(ang)
    sin_ref[...] = jnp.sin(ang)


def _rope_tables(positions):
    b, s = positions.shape
    m = b * s
    half = QK_ROPE // 2
    per_row = LANES // half
    rows = m // per_row
    blk = min(rows, 1024)
    inv_freq = ROPE_THETA ** (-jnp.arange(half, dtype=F32) / half)
    invf = jnp.tile(inv_freq, per_row).reshape(1, LANES)
    pos_rep = jnp.repeat(positions.reshape(m), half).reshape(rows, LANES)
    cos, sin = pl.pallas_call(
        _rope_kernel,
        grid=(rows // blk,),
        in_specs=[pl.BlockSpec((blk, LANES), lambda i: (i, 0)),
                  pl.BlockSpec((1, LANES), lambda i: (0, 0))],
        out_specs=[pl.BlockSpec((blk, LANES), lambda i: (i, 0)),
                   pl.BlockSpec((blk, LANES), lambda i: (i, 0))],
        out_shape=[jax.ShapeDtypeStruct((rows, LANES), F32)] * 2,
        compiler_params=_cparams(("parallel",)),
        name="rope_table",
    )(pos_rep, invf)
    c = cos.reshape(m, half)
    sn = sin.reshape(m, half)
    zero = jnp.zeros((m, LANES - QK_ROPE), F32)
    cos_tab = jnp.concatenate([c, c, zero], axis=1).reshape(b, s, LANES)
    sin_tab = jnp.concatenate([-sn, sn, zero], axis=1).reshape(b, s, LANES)
    return cos_tab, sin_tab


def _rotate_half_unsigned(x, lane):
    half = QK_ROPE // 2
    return jnp.where(lane < half, pltpu.roll(x, LANES - half, axis=1), pltpu.roll(x, half, axis=1))


def _inproj_kernel(x_ref, nw_ref, w_ref, o_ref, small_ref, h_ref, *, nj, small_off):
    j = pl.program_id(1)

    @pl.when(j == 0)
    def _():
        x = x_ref[...]
        ms = jnp.mean(x * x, axis=-1, keepdims=True)
        h_ref[...] = (x * lax.rsqrt(ms + EPS) * nw_ref[...]).astype(BF16)

    acc = _dot(h_ref[...], w_ref[...])
    o_ref[...] = acc.astype(BF16)

    @pl.when(j == nj - 1)
    def _():
        small_ref[...] = acc[:, small_off:small_off + LANES]


def _in_projection(x2, norm_w, w_in):
    m, d = x2.shape
    conv_ch = 3 * GDN_HEADS * HEAD_DIM
    gdn_v = GDN_HEADS * HEAD_DIM
    o = 0
    qkv = w_in[:, o:o + conv_ch]; o += conv_ch
    z = w_in[:, o:o + gdn_v]; o += gdn_v
    b_raw = w_in[:, o:o + GDN_HEADS]; o += GDN_HEADS
    a_raw = w_in[:, o:o + GDN_HEADS]; o += GDN_HEADS
    cq = w_in[:, o:o + Q_LORA]; o += Q_LORA
    ckv = w_in[:, o:o + KV_LORA]; o += KV_LORA
    kr = w_in[:, o:o + QK_ROPE]
    main_w = conv_ch + gdn_v + Q_LORA + KV_LORA
    small_used = QK_ROPE + 2 * GDN_HEADS
    tn = INPROJ_TN
    total = pl.cdiv(main_w + LANES, tn) * tn
    pad = jnp.zeros((d, total - main_w - small_used), w_in.dtype)
    w = jnp.concatenate([qkv, z, cq, ckv, kr, b_raw, a_raw, pad], axis=1).astype(BF16)
    nj = total // tn
    small_off = main_w - (nj - 1) * tn
    tm = INPROJ_TM
    kern = functools.partial(_inproj_kernel, nj=nj, small_off=small_off)
    proj, small = pl.pallas_call(
        kern,
        grid=(m // tm, nj),
        in_specs=[pl.BlockSpec((tm, d), lambda i, j: (i, 0)),
                  pl.BlockSpec((1, d), lambda i, j: (0, 0)),
                  pl.BlockSpec((d, tn), lambda i, j: (0, j))],
        out_specs=[pl.BlockSpec((tm, tn), lambda i, j: (i, j)),
                   pl.BlockSpec((tm, LANES), lambda i, j: (i, 0))],
        out_shape=[jax.ShapeDtypeStruct((m, total), BF16),
                   jax.ShapeDtypeStruct((m, LANES), F32)],
        scratch_shapes=[pltpu.VMEM((tm, d), BF16)],
        compiler_params=_cparams(("parallel", "arbitrary")),
        name="in_projection",
    )(x2, norm_w.reshape(1, d), w)
    return proj, small


def _gdn_kernel(alog_ref, dtb_ref, q_ref, k_ref, v_ref, z_ref, a_ref, b_ref,
                cwq_ref, cwk_ref, cwv_ref, nw_ref, o_ref,
                state_ref, xq_ref, xk_ref, xv_ref, qn_ref, kn_ref, vv_ref, gc_ref, beta_ref, oacc_ref):
    t = pl.program_id(1)
    T = q_ref.shape[1]
    nh = GDN_HEADS
    hist = SUBLANES
    blk = GDN_BLK
    c = GDN_CHUNK

    @pl.when(t == 0)
    def _():
        state_ref[...] = jnp.zeros_like(state_ref)
        zero_hist = jnp.zeros((hist, nh * HEAD_DIM), F32)
        xq_ref[0:hist, :] = zero_hist
        xk_ref[0:hist, :] = zero_hist
        xv_ref[0:hist, :] = zero_hist

    def conv_silu(in_ref, xbuf, cw_ref):
        xbuf[hist:hist + T, :] = in_ref[0].astype(F32)
        w = cw_ref[...]
        y = xbuf[hist - 3:hist - 3 + T, :] * w[0:1, :]
        for i in range(1, GDN_CONV):
            y = y + xbuf[hist - 3 + i:hist - 3 + i + T, :] * w[i:i + 1, :]
        xbuf[0:hist, :] = xbuf[T:T + hist, :]
        return y * _sigmoid(y)

    q = conv_silu(q_ref, xq_ref, cwq_ref)
    k = conv_silu(k_ref, xk_ref, cwk_ref)
    vv_ref[...] = conv_silu(v_ref, xv_ref, cwv_ref)
    for hd in range(nh):
        sl = slice(hd * HEAD_DIM, (hd + 1) * HEAD_DIM)
        qh = q[:, sl]
        kh = k[:, sl]
        qn_ref[:, sl] = qh * lax.rsqrt(jnp.sum(qh * qh, axis=-1, keepdims=True) + EPS) * (HEAD_DIM ** -0.5)
        kn_ref[:, sl] = kh * lax.rsqrt(jnp.sum(kh * kh, axis=-1, keepdims=True) + EPS)

    neg_a = -jnp.exp(alog_ref[...])
    dtb = dtb_ref[...]
    lane = lax.broadcasted_iota(jnp.int32, (nh, LANES), 1)
    in_chunk = lane % c
    for r in range(T // LANES):
        tile = pl.ds(r * nh, nh)
        beta_ref[tile, :] = _sigmoid(b_ref[0, r])
        gc = neg_a * _softplus(a_ref[0, r] + dtb)
        s = 1
        while s < c:
            gc = gc + jnp.where(in_chunk >= s, pltpu.roll(gc, s, axis=1), 0.0)
            s *= 2
        gc_ref[tile, :] = gc

    row = lax.broadcasted_iota(jnp.int32, (blk, blk), 0)
    col = lax.broadcasted_iota(jnp.int32, (blk, blk), 1)
    same_chunk = (row // c) == (col // c)
    m_tril = jnp.logical_and(same_chunk, row >= col)
    m_strict = jnp.logical_and(same_chunk, row > col)
    eye = jnp.where(row == col, 1.0, 0.0).astype(F32)
    first = row < c

    heads = range(nh)
    sls = [slice(hd * HEAD_DIM, (hd + 1) * HEAD_DIM) for hd in heads]
    bf = lambda xs: [x.astype(BF16) for x in xs]

    def block(r, carry):
        rows = pl.ds(pl.multiple_of(r * blk, blk), blk)
        qn = [qn_ref[rows, sl] for sl in sls]
        kn = [kn_ref[rows, sl] for sl in sls]
        v = [vv_ref[rows, sl] for sl in sls]
        g_row = [jnp.broadcast_to(gc_ref[pl.ds(r * nh + hd, 1), :], (blk, blk)) for hd in heads]
        g_col = [x.T for x in g_row]
        b_col = [jnp.broadcast_to(beta_ref[pl.ds(r * nh + hd, 1), :], (blk, blk)).T for hd in heads]
        decay = [jnp.exp(jnp.where(m_tril, gc_ - gr_, -jnp.inf)) for gc_, gr_ in zip(g_col, g_row)]
        kb = [a * b for a, b in zip(kn, b_col)]
        vb = [a * b for a, b in zip(v, b_col)]
        kn16 = bf(kn)
        kk = [_dot_nt(a, b) for a, b in zip(bf(kb), kn16)]
        qk = [_dot_nt(a, b) for a, b in zip(bf(qn), kn16)]
        a_mat = [jnp.where(m_strict, x * d, 0.0) for x, d in zip(kk, decay)]
        intra16 = bf([x * d for x, d in zip(qk, decay)])
        inv = [eye - a for a in a_mat]
        p16 = bf(a_mat)
        n = 2
        while n < c:
            p16 = bf([_dot(x, x) for x in p16])
            inv = [i + _dot(i16, x) for i, i16, x in zip(inv, bf(inv), p16)]
            n *= 2
        eg = [jnp.exp(x) for x in g_col]
        rhs = bf([jnp.concatenate([a, b * e], axis=1) for a, b, e in zip(vb, kb, eg)])
        uw = [_dot(i16, x) for i16, x in zip(bf(inv), rhs)]
        u = [x[:, :HEAD_DIM] for x in uw]
        w16 = bf([x[:, HEAD_DIM:] for x in uw])
        g_last1 = [jnp.broadcast_to(x[c - 1:c, :], (blk, blk)) for x in g_col]
        g_last2 = [jnp.broadcast_to(x[blk - 1:blk, :], (blk, blk)) for x in g_col]
        kdec16 = bf([k_ * jnp.exp(jnp.where(first, l1, l2) - gc_)
                     for k_, l1, l2, gc_ in zip(kn, g_last1, g_last2, g_col)])
        qg16 = bf([a * e for a, e in zip(qn, eg)])
        e1 = [jnp.exp(x) for x in g_last1]
        e2 = [jnp.exp(x) for x in g_last2]

        s0 = [state_ref[hd] for hd in heads]
        s016 = bf(s0)
        vn1 = [u_[0:c] - _dot(w_[0:c], s_) for u_, w_, s_ in zip(u, w16, s016)]
        oq1 = [_dot(q_[0:c], s_) for q_, s_ in zip(qg16, s016)]
        s1 = [s_ * e_ + _dot_tn(k_[0:c], x) for s_, e_, k_, x in zip(s0, e1, kdec16, bf(vn1))]
        s116 = bf(s1)
        vn2 = [u_[c:blk] - _dot(w_[c:blk], s_) for u_, w_, s_ in zip(u, w16, s116)]
        oq2 = [_dot(q_[c:blk], s_) for q_, s_ in zip(qg16, s116)]
        s2 = [s_ * e_ + _dot_tn(k_[c:blk], x) for s_, e_, k_, x in zip(s1, e2, kdec16, bf(vn2))]
        for hd in heads:
            state_ref[hd] = s2[hd]
            vn = jnp.concatenate([vn1[hd], vn2[hd]], axis=0).astype(BF16)
            oacc_ref[rows, sls[hd]] = (jnp.concatenate([oq1[hd], oq2[hd]], axis=0)
                                       + _dot(intra16[hd], vn))
        return carry

    lax.fori_loop(0, T // blk, block, 0)

    nw = nw_ref[...]
    for hd in range(nh):
        sl = slice(hd * HEAD_DIM, (hd + 1) * HEAD_DIM)
        o = oacc_ref[:, sl]
        zz = z_ref[0, :, sl].astype(F32)
        y = o * lax.rsqrt(jnp.mean(o * o, axis=-1, keepdims=True) + EPS) * nw
        o_ref[0, :, sl] = (y * (zz * _sigmoid(zz))).astype(o_ref.dtype)


def _gated_delta(proj3, small3, conv_w, a_log, dt_bias, norm_w):
    b, s, _ = proj3.shape
    hh = GDN_HEADS
    T = GDN_T
    tiles = T // LANES
    width = hh * HEAD_DIM
    gate_tiles = lambda g: g.reshape(b, s // LANES, LANES, hh).transpose(0, 1, 3, 2)
    b_t = gate_tiles(small3[:, :, QK_ROPE:QK_ROPE + hh])
    a_t = gate_tiles(small3[:, :, QK_ROPE + hh:QK_ROPE + 2 * hh])
    lane_bcast = lambda p: jnp.broadcast_to(p.reshape(hh, 1), (hh, LANES))
    col_spec = lambda j: pl.BlockSpec((1, T, width), lambda bi, ti: (bi, ti, j))
    gate_spec = pl.BlockSpec((1, tiles, hh, LANES), lambda bi, ti: (bi, ti, 0, 0))
    cw_spec = lambda j: pl.BlockSpec((GDN_CONV, width), lambda bi, ti: (0, j))
    head_spec = pl.BlockSpec((hh, LANES), lambda bi, ti: (0, 0))
    buf = pltpu.VMEM((T + SUBLANES, width), F32)
    full = pltpu.VMEM((T, width), F32)
    gate_buf = pltpu.VMEM((tiles * hh, LANES), F32)
    return pl.pallas_call(
        _gdn_kernel,
        grid=(b, s // T),
        in_specs=[head_spec, head_spec,
                  col_spec(0), col_spec(1), col_spec(2), col_spec(3),
                  gate_spec, gate_spec,
                  cw_spec(0), cw_spec(1), cw_spec(2),
                  pl.BlockSpec((1, HEAD_DIM), lambda bi, ti: (0, 0))],
        out_specs=pl.BlockSpec((1, T, width), lambda bi, ti: (bi, ti, 0)),
        out_shape=jax.ShapeDtypeStruct((b, s, width), BF16),
        scratch_shapes=[pltpu.VMEM((hh, HEAD_DIM, HEAD_DIM), F32), buf, buf, buf, full, full, full,
                        gate_buf, gate_buf, full],
        compiler_params=_cparams(("parallel", "arbitrary")),
        name="gated_delta",
    )(lane_bcast(a_log), lane_bcast(dt_bias), proj3, proj3, proj3, proj3, a_t, b_t,
      conv_w, conv_w, conv_w, norm_w.reshape(1, HEAD_DIM))


def _rms_rows(x, w):
    ms = jnp.mean(x * x, axis=-1, keepdims=True)
    return x * lax.rsqrt(ms + EPS) * w


def _qproj_kernel(c_ref, nw_ref, w_ref, cos_ref, sin_ref, o_ref):
    hq = _rms_rows(c_ref[0].astype(F32), nw_ref[...]).astype(BF16)
    res = _dot(hq, w_ref[...])
    cos = cos_ref[0]
    sin = sin_ref[0]
    lane = lax.broadcasted_iota(jnp.int32, cos.shape, 1)
    scale = SOFTMAX_SCALE * LOG2E
    for hd in range(MLA_HEADS):
        base = hd * QK_PAD
        nope = res[:, base:base + QK_NOPE]
        rp = res[:, base + QK_NOPE:base + QK_PAD]
        rot = rp * cos + _rotate_half_unsigned(rp, lane) * sin
        o_ref[0, hd, :, 0:QK_NOPE] = (nope * scale).astype(BF16)
        o_ref[0, hd, :, QK_NOPE:QK_PAD] = (rot * scale).astype(BF16)


def _kvproj_kernel(c_ref, small_ref, nw_ref, w_ref, cos_ref, sin_ref, k_ref, v_ref):
    hkv = _rms_rows(c_ref[0].astype(F32), nw_ref[...]).astype(BF16)
    res = _dot(hkv, w_ref[...])
    cos = cos_ref[0]
    sin = sin_ref[0]
    lane = lax.broadcasted_iota(jnp.int32, cos.shape, 1)
    kr = small_ref[0]
    kr16 = (kr * cos + _rotate_half_unsigned(kr, lane) * sin).astype(BF16)
    nk = MLA_HEADS * QK_NOPE
    ones = jnp.ones((res.shape[0], V_HEAD), BF16)
    for hd in range(MLA_HEADS):
        k_ref[0, hd, :, 0:QK_NOPE] = res[:, hd * QK_NOPE:(hd + 1) * QK_NOPE].astype(BF16)
        k_ref[0, hd, :, QK_NOPE:QK_PAD] = kr16
        v_ref[0, hd, :, 0:V_HEAD] = res[:, nk + hd * V_HEAD:nk + (hd + 1) * V_HEAD].astype(BF16)
        v_ref[0, hd, :, V_HEAD:2 * V_HEAD] = ones


def _mla_projections(proj3, small3, q_norm_w, w_uq, kv_norm_w, w_ukv, cos_tab, sin_tab):
    b, s, _ = proj3.shape
    hh = MLA_HEADS
    tm = MLA_TM
    main_cq = (3 * GDN_HEADS * HEAD_DIM + GDN_HEADS * HEAD_DIM) // Q_LORA
    main_ckv = main_cq + 1
    wq = w_uq.reshape(Q_LORA, hh, QK_NOPE + QK_ROPE)
    wq = jnp.concatenate([wq, jnp.zeros((Q_LORA, hh, QK_PAD - QK_NOPE - QK_ROPE), w_uq.dtype)], axis=2)
    wq = wq.reshape(Q_LORA, hh * QK_PAD).astype(BF16)
    wkv = w_ukv.reshape(KV_LORA, hh, QK_NOPE + V_HEAD)
    wkv = jnp.concatenate([wkv[:, :, :QK_NOPE].reshape(KV_LORA, hh * QK_NOPE),
                           wkv[:, :, QK_NOPE:].reshape(KV_LORA, hh * V_HEAD)], axis=1).astype(BF16)
    tab_spec = pl.BlockSpec((1, tm, LANES), lambda bi, ti: (bi, ti, 0))
    q = pl.pallas_call(
        _qproj_kernel,
        grid=(b, s // tm),
        in_specs=[pl.BlockSpec((1, tm, Q_LORA), lambda bi, ti: (bi, ti, main_cq)),
                  pl.BlockSpec((1, Q_LORA), lambda bi, ti: (0, 0)),
                  pl.BlockSpec((Q_LORA, hh * QK_PAD), lambda bi, ti: (0, 0)),
                  tab_spec, tab_spec],
        out_specs=pl.BlockSpec((1, hh, tm, QK_PAD), lambda bi, ti: (bi, 0, ti, 0)),
        out_shape=jax.ShapeDtypeStruct((b, hh, s, QK_PAD), BF16),
        compiler_params=_cparams(("parallel", "parallel")),
        name="mla_q_proj",
    )(proj3, q_norm_w.reshape(1, Q_LORA), wq, cos_tab, sin_tab)
    k, v = pl.pallas_call(
        _kvproj_kernel,
        grid=(b, s // tm),
        in_specs=[pl.BlockSpec((1, tm, KV_LORA), lambda bi, ti: (bi, ti, main_ckv)),
                  tab_spec,
                  pl.BlockSpec((1, KV_LORA), lambda bi, ti: (0, 0)),
                  pl.BlockSpec((KV_LORA, hh * (QK_NOPE + V_HEAD)), lambda bi, ti: (0, 0)),
                  tab_spec, tab_spec],
        out_specs=[pl.BlockSpec((1, hh, tm, QK_PAD), lambda bi, ti: (bi, 0, ti, 0)),
                   pl.BlockSpec((1, hh, tm, 2 * V_HEAD), lambda bi, ti: (bi, 0, ti, 0))],
        out_shape=[jax.ShapeDtypeStruct((b, hh, s, QK_PAD), BF16),
                   jax.ShapeDtypeStruct((b, hh, s, 2 * V_HEAD), BF16)],
        compiler_params=_cparams(("parallel", "parallel")),
        name="mla_kv_proj",
    )(proj3, small3, kv_norm_w.reshape(1, KV_LORA), wkv, cos_tab, sin_tab)
    return q, k, v


def _attn_kernel(q_ref, k_ref, v_ref, nw_ref, o_ref, sa_ref, sb_ref, m_ref, acc_ref):
    i = pl.program_id(2)
    tq = q_ref.shape[2]
    tk = tq
    q = q_ref[0, 0]
    m_ref[...] = jnp.full(m_ref.shape, -jnp.inf, F32)
    acc_ref[...] = jnp.zeros(acc_ref.shape, F32)

    def scores(j):
        kb = k_ref[0, 0, pl.ds(pl.multiple_of(j * tk, tk), tk), :]
        return _dot_nt(q, kb)

    def update(sc, j):
        vb = v_ref[0, 0, pl.ds(pl.multiple_of(j * tk, tk), tk), :]
        m_prev = m_ref[...]
        m_new = jnp.maximum(m_prev, jnp.max(sc, axis=-1, keepdims=True))
        p = jnp.exp2(sc - jnp.concatenate([m_new] * (tk // LANES), axis=1))
        alpha = jnp.exp2(m_prev - m_new)
        acc_ref[...] = jnp.concatenate([alpha, alpha], axis=1) * acc_ref[...] + _dot(p.astype(BF16), vb)
        m_ref[...] = m_new

    def update_diagonal(sc):
        qpos = lax.broadcasted_iota(jnp.int32, sc.shape, 0)
        kpos = lax.broadcasted_iota(jnp.int32, sc.shape, 1)
        update(jnp.where(kpos <= qpos, sc, -jnp.inf), i)

    sa_ref[...] = scores(0)

    def body(jj, carry):
        j = 2 * jj
        sb_ref[...] = scores(j + 1)
        update(sa_ref[...], j)
        sa_ref[...] = scores(j + 2)
        update(sb_ref[...], j + 1)
        return carry

    lax.fori_loop(0, i // 2, body, 0)

    @pl.when(i % 2 == 0)
    def _():
        update_diagonal(sa_ref[...])

    @pl.when(i % 2 == 1)
    def _():
        sb_ref[...] = scores(i)
        update(sa_ref[...], i - 1)
        update_diagonal(sb_ref[...])

    acc = acc_ref[...]
    o = acc[:, :V_HEAD] / acc[:, V_HEAD:]
    y = o * lax.rsqrt(jnp.mean(o * o, axis=-1, keepdims=True) + EPS) * nw_ref[...]
    o_ref[0] = y.astype(o_ref.dtype)


def _attention(q, k, v1, norm_w):
    b, hh, s, _ = q.shape
    tq = ATT_TQ
    return pl.pallas_call(
        _attn_kernel,
        grid=(b, hh, s // tq),
        in_specs=[pl.BlockSpec((1, 1, tq, QK_PAD), lambda bi, hi, qi: (bi, hi, qi, 0)),
                  pl.BlockSpec((1, 1, s, QK_PAD), lambda bi, hi, qi: (bi, hi, 0, 0)),
                  pl.BlockSpec((1, 1, s, 2 * V_HEAD), lambda bi, hi, qi: (bi, hi, 0, 0)),
                  pl.BlockSpec((1, V_HEAD), lambda bi, hi, qi: (0, 0))],
        out_specs=pl.BlockSpec((1, tq, V_HEAD), lambda bi, hi, qi: (bi, qi, hi)),
        out_shape=jax.ShapeDtypeStruct((b, s, hh * V_HEAD), BF16),
        scratch_shapes=[pltpu.VMEM((tq, tq), F32), pltpu.VMEM((tq, tq), F32), pltpu.VMEM((tq, V_HEAD), F32),
                        pltpu.VMEM((tq, 2 * V_HEAD), F32)],
        compiler_params=_cparams(("parallel", "parallel", "arbitrary")),
        name="mla_attention",
    )(q, k, v1, norm_w.reshape(1, V_HEAD))


def _outproj_kernel(og_ref, om_ref, x_ref, wg_ref, wm_ref, o_ref):
    o_ref[...] = x_ref[...] + _dot(og_ref[...], wg_ref[...]) + _dot(om_ref[...], wm_ref[...])


def _out_projection(o_gdn, o_mla, x2, w_out):
    m, d = x2.shape
    kg = o_gdn.shape[1]
    km = o_mla.shape[1]
    tm = OUT_TM
    w16 = w_out.astype(BF16)
    return pl.pallas_call(
        _outproj_kernel,
        grid=(m // tm,),
        in_specs=[pl.BlockSpec((tm, kg), lambda i: (i, 0)),
                  pl.BlockSpec((tm, km), lambda i: (i, 0)),
                  pl.BlockSpec((tm, d), lambda i: (i, 0)),
                  pl.BlockSpec((kg, d), lambda i: (0, 0)),
                  pl.BlockSpec((km, d), lambda i: (0, 0))],
        out_specs=pl.BlockSpec((tm, d), lambda i: (i, 0)),
        out_shape=jax.ShapeDtypeStruct((m, d), F32),
        compiler_params=_cparams(("parallel",)),
        name="out_projection",
    )(o_gdn, o_mla, x2, w16[:kg], w16[kg:])


def _ffn_kernel(x_ref, nw_ref, wg_ref, wu_ref, wd_ref, fw_ref, o_ref, h_ref, *, nj):
    j = pl.program_id(1)

    @pl.when(j == 0)
    def _():
        h_ref[...] = _rms_rows(x_ref[...], nw_ref[...]).astype(BF16)

    h = h_ref[...]
    g = _dot(h, wg_ref[...])
    u = _dot(h, wu_ref[...])
    a = (g * _sigmoid(g) * u).astype(BF16)
    part = _dot(a, wd_ref[...])

    @pl.when(j == 0)
    def _():
        o_ref[...] = part

    @pl.when(j > 0)
    def _():
        o_ref[...] += part

    @pl.when(j == nj - 1)
    def _():
        o_ref[...] = _rms_rows(x_ref[...] + o_ref[...], fw_ref[...])


def _ffn(x1, norm_w, w_gate, w_up, w_down, final_w):
    m, d = x1.shape
    dff = w_gate.shape[1]
    tm = FFN_TM
    tf = FFN_TF
    nj = dff // tf
    kern = functools.partial(_ffn_kernel, nj=nj)
    return pl.pallas_call(
        kern,
        grid=(m // tm, nj),
        in_specs=[pl.BlockSpec((tm, d), lambda i, j: (i, 0)),
                  pl.BlockSpec((1, d), lambda i, j: (0, 0)),
                  pl.BlockSpec((d, tf), lambda i, j: (0, j)),
                  pl.BlockSpec((d, tf), lambda i, j: (0, j)),
                  pl.BlockSpec((tf, d), lambda i, j: (j, 0)),
                  pl.BlockSpec((1, d), lambda i, j: (0, 0))],
        out_specs=pl.BlockSpec((tm, d), lambda i, j: (i, 0)),
        out_shape=jax.ShapeDtypeStruct((m, d), F32),
        scratch_shapes=[pltpu.VMEM((tm, d), BF16)],
        compiler_params=_cparams(("parallel", "arbitrary")),
        name="swiglu_ffn",
    )(x1, norm_w.reshape(1, d), w_gate.astype(BF16), w_up.astype(BF16), w_down.astype(BF16),
      final_w.reshape(1, d))


def kernel(x, positions, attn_norm_w, w_in, conv_w, a_log, dt_bias, gdn_norm_w, q_norm_w, w_uq,
           kv_norm_w, w_ukv, mla_out_norm_w, w_out, ffn_norm_w, w_gate, w_up, w_down, final_norm_w):
    b, s, d = x.shape
    assert w_in.shape[0] == 1, "the final rmsnorm is fused into the (single) layer's FFN kernel"
    l = 0
    cos_tab, sin_tab = _rope_tables(positions)
    x2 = x.reshape(b * s, d)
    proj, small = _in_projection(x2, attn_norm_w[l], w_in[l])
    proj3 = proj.reshape(b, s, proj.shape[1])
    small3 = small.reshape(b, s, LANES)
    o_gdn = _gated_delta(proj3, small3, conv_w[l], a_log[l], dt_bias[l], gdn_norm_w[l])
    q, k, v = _mla_projections(proj3, small3, q_norm_w[l], w_uq[l], kv_norm_w[l], w_ukv[l],
                               cos_tab, sin_tab)
    o_mla = _attention(q, k, v, mla_out_norm_w[l])
    x1 = _out_projection(o_gdn.reshape(b * s, -1), o_mla.reshape(b * s, -1), x2, w_out[l])
    out = _ffn(x1, ffn_norm_w[l], w_gate[l], w_up[l], w_down[l], final_norm_w)
    return out.reshape(b, s, d)
```

```python
import functools
import math

import jax
import jax.numpy as jnp
from jax import lax
from jax.experimental import pallas as pl
from jax.experimental.pallas import tpu as pltpu

F32 = jnp.float32
BF16 = jnp.bfloat16

GDN_HEADS = 8
HEAD_DIM = 128
GDN_CONV = 4
GDN_CHUNK = 64
MLA_HEADS = 8
QK_NOPE = 128
QK_ROPE = 64
V_HEAD = 128
Q_LORA = 512
KV_LORA = 512
ROPE_THETA = 10000.0
EPS = 1e-6

LANES = 128
SUBLANES = 8
VMEM_LIMIT_BYTES = 56 * 2**20

INPROJ_TM = 1024
INPROJ_TN = 1792
GDN_T = 512
GDN_BLK = 2 * GDN_CHUNK
GDN_UNROLL = 2
MLA_TM = 512
ATT_TQ = 512
ATT_HEADS = 2
OUT_TM = 512
FFN_TM = 1024
FFN_TF = 512

QK_PAD = 256
SOFTMAX_SCALE = (QK_NOPE + QK_ROPE) ** -0.5
LOG2E = math.log2(math.e)


def _cparams(semantics):
    return pltpu.CompilerParams(dimension_semantics=semantics, vmem_limit_bytes=VMEM_LIMIT_BYTES)


def _sigmoid(x):
    return 1.0 / (1.0 + jnp.exp(-x))


def _softplus(x):
    return jnp.maximum(x, 0.0) + jnp.log1p(jnp.exp(-jnp.abs(x)))


def _dot(a, b):
    return jnp.dot(a, b, preferred_element_type=F32)


def _dot_nt(a, b):
    return lax.dot_general(a, b, (((1,), (1,)), ((), ())), preferred_element_type=F32)


def _dot_tn(a, b):
    return lax.dot_general(a, b, (((0,), (0,)), ((), ())), preferred_element_type=F32)


def _rope_kernel(pos_ref, invf_ref, cos_ref, sin_ref):
    ang = pos_ref[...].astype(F32) * invf_ref[...]
    cos_ref[...] = jnp.cos(ang)
    sin_ref[...] = jnp.sin(ang)


def _rope_tables(positions):
    b, s = positions.shape
    m = b * s
    half = QK_ROPE // 2
    per_row = LANES // half
    rows = m // per_row
    blk = min(rows, 1024)
    inv_freq = ROPE_THETA ** (-jnp.arange(half, dtype=F32) / half)
    invf = jnp.tile(inv_freq, per_row).reshape(1, LANES)
    pos_rep = jnp.repeat(positions.reshape(m), half).reshape(rows, LANES)
    cos, sin = pl.pallas_call(
        _rope_kernel,
        grid=(rows // blk,),
        in_specs=[pl.BlockSpec((blk, LANES), lambda i: (i, 0)),
                  pl.BlockSpec((1, LANES), lambda i: (0, 0))],
        out_specs=[pl.BlockSpec((blk, LANES), lambda i: (i, 0)),
                   pl.BlockSpec((blk, LANES), lambda i: (i, 0))],
        out_shape=[jax.ShapeDtypeStruct((rows, LANES), F32)] * 2,
        compiler_params=_cparams(("parallel",)),
        name="rope_table",
    )(pos_rep, invf)
    c = cos.reshape(m, half)
    sn = sin.reshape(m, half)
    zero = jnp.zeros((m, LANES - QK_ROPE), F32)
    cos_tab = jnp.concatenate([c, c, zero], axis=1).reshape(b, s, LANES)
    sin_tab = jnp.concatenate([-sn, sn, zero], axis=1).reshape(b, s, LANES)
    return cos_tab, sin_tab


def _rotate_half_unsigned(x, lane):
    half = QK_ROPE // 2
    return jnp.where(lane < half, pltpu.roll(x, LANES - half, axis=1), pltpu.roll(x, half, axis=1))


def _inproj_kernel(x_ref, nw_ref, w_ref, o_ref, small_ref, h_ref, *, nj, small_off):
    j = pl.program_id(1)

    @pl.when(j == 0)
    def _():
        x = x_ref[...]
        ms = jnp.mean(x * x, axis=-1, keepdims=True)
        h_ref[...] = (x * lax.rsqrt(ms + EPS) * nw_ref[...]).astype(BF16)

    acc = _dot(h_ref[...], w_ref[...])
    o_ref[...] = acc.astype(BF16)

    @pl.when(j == nj - 1)
    def _():
        small_ref[...] = acc[:, small_off:small_off + LANES]


def _in_projection(x2, norm_w, w_in):
    m, d = x2.shape
    conv_ch = 3 * GDN_HEADS * HEAD_DIM
    gdn_v = GDN_HEADS * HEAD_DIM
    o = 0
    qkv = w_in[:, o:o + conv_ch]; o += conv_ch
    z = w_in[:, o:o + gdn_v]; o += gdn_v
    b_raw = w_in[:, o:o + GDN_HEADS]; o += GDN_HEADS
    a_raw = w_in[:, o:o + GDN_HEADS]; o += GDN_HEADS
    cq = w_in[:, o:o + Q_LORA]; o += Q_LORA
    ckv = w_in[:, o:o + KV_LORA]; o += KV_LORA
    kr = w_in[:, o:o + QK_ROPE]
    main_w = conv_ch + gdn_v + Q_LORA + KV_LORA
    small_used = QK_ROPE + 2 * GDN_HEADS
    tn = INPROJ_TN
    total = pl.cdiv(main_w + LANES, tn) * tn
    pad = jnp.zeros((d, total - main_w - small_used), w_in.dtype)
    w = jnp.concatenate([qkv, z, cq, ckv, kr, b_raw, a_raw, pad], axis=1).astype(BF16)
    nj = total // tn
    small_off = main_w - (nj - 1) * tn
    tm = INPROJ_TM
    kern = functools.partial(_inproj_kernel, nj=nj, small_off=small_off)
    proj, small = pl.pallas_call(
        kern,
        grid=(m // tm, nj),
        in_specs=[pl.BlockSpec((tm, d), lambda i, j: (i, 0)),
                  pl.BlockSpec((1, d), lambda i, j: (0, 0)),
                  pl.BlockSpec((d, tn), lambda i, j: (0, j))],
        out_specs=[pl.BlockSpec((tm, tn), lambda i, j: (i, j)),
                   pl.BlockSpec((tm, LANES), lambda i, j: (i, 0))],
        out_shape=[jax.ShapeDtypeStruct((m, total), BF16),
                   jax.ShapeDtypeStruct((m, LANES), F32)],
        scratch_shapes=[pltpu.VMEM((tm, d), BF16)],
        compiler_params=_cparams(("parallel", "arbitrary")),
        name="in_projection",
    )(x2, norm_w.reshape(1, d), w)
    return proj, small


def _gdn_kernel(alog_ref, dtb_ref, q_ref, k_ref, v_ref, z_ref, a_ref, b_ref,
                cwq_ref, cwk_ref, cwv_ref, nw_ref, o_ref,
                state_ref, xq_ref, xk_ref, xv_ref, qn_ref, kn_ref, vv_ref, gc_ref, beta_ref, oacc_ref):
    t = pl.program_id(1)
    T = q_ref.shape[1]
    nh = GDN_HEADS
    hist = SUBLANES
    blk = GDN_BLK
    c = GDN_CHUNK

    @pl.when(t == 0)
    def _():
        state_ref[...] = jnp.zeros_like(state_ref)
        zero_hist = jnp.zeros((hist, nh * HEAD_DIM), F32)
        xq_ref[0:hist, :] = zero_hist
        xk_ref[0:hist, :] = zero_hist
        xv_ref[0:hist, :] = zero_hist

    def conv_silu(in_ref, xbuf, cw_ref):
        xbuf[hist:hist + T, :] = in_ref[0].astype(F32)
        w = cw_ref[...]
        y = xbuf[hist - 3:hist - 3 + T, :] * w[0:1, :]
        for i in range(1, GDN_CONV):
            y = y + xbuf[hist - 3 + i:hist - 3 + i + T, :] * w[i:i + 1, :]
        xbuf[0:hist, :] = xbuf[T:T + hist, :]
        return y * _sigmoid(y)

    q = conv_silu(q_ref, xq_ref, cwq_ref)
    k = conv_silu(k_ref, xk_ref, cwk_ref)
    vv_ref[...] = conv_silu(v_ref, xv_ref, cwv_ref)
    for hd in range(nh):
        sl = slice(hd * HEAD_DIM, (hd + 1) * HEAD_DIM)
        qh = q[:, sl]
        kh = k[:, sl]
        qn_ref[:, sl] = qh * lax.rsqrt(jnp.sum(qh * qh, axis=-1, keepdims=True) + EPS) * (HEAD_DIM ** -0.5)
        kn_ref[:, sl] = kh * lax.rsqrt(jnp.sum(kh * kh, axis=-1, keepdims=True) + EPS)

    neg_a = -jnp.exp(alog_ref[...])
    dtb = dtb_ref[...]
    lane = lax.broadcasted_iota(jnp.int32, (nh, LANES), 1)
    in_chunk = lane % c
    for r in range(T // LANES):
        tile = pl.ds(r * nh, nh)
        beta_ref[tile, :] = _sigmoid(b_ref[0, r])
        gc = neg_a * _softplus(a_ref[0, r] + dtb)
        s = 1
        while s < c:
            gc = gc + jnp.where(in_chunk >= s, pltpu.roll(gc, s, axis=1), 0.0)
            s *= 2
        gc_ref[tile, :] = gc

    row = lax.broadcasted_iota(jnp.int32, (blk, blk), 0)
    col = lax.broadcasted_iota(jnp.int32, (blk, blk), 1)
    same_chunk = (row // c) == (col // c)
    m_tril = jnp.logical_and(same_chunk, row >= col)
    m_strict = jnp.logical_and(same_chunk, row > col)
    eye = jnp.where(row == col, 1.0, 0.0).astype(F32)
    first = row < c

    heads = range(nh)
    sls = [slice(hd * HEAD_DIM, (hd + 1) * HEAD_DIM) for hd in heads]
    bf = lambda xs: [x.astype(BF16) for x in xs]

    unroll = GDN_UNROLL
    streams = [(sb, hd) for sb in range(unroll) for hd in heads]

    def block(rr, carry):
        rows = [pl.ds(pl.multiple_of((rr * unroll + sb) * blk, blk), blk) for sb in range(unroll)]
        gate = lambda sb, hd: pl.ds((rr * unroll + sb) * nh + hd, 1)
        qn = [qn_ref[rows[sb], sls[hd]] for sb, hd in streams]
        kn = [kn_ref[rows[sb], sls[hd]] for sb, hd in streams]
        v = [vv_ref[rows[sb], sls[hd]] for sb, hd in streams]
        g_row = [jnp.broadcast_to(gc_ref[gate(sb, hd), :], (blk, blk)) for sb, hd in streams]
        g_col = [x.T for x in g_row]
        b_col = [jnp.broadcast_to(beta_ref[gate(sb, hd), :], (blk, blk)).T for sb, hd in streams]
        decay = [jnp.exp(jnp.where(m_tril, gc_ - gr_, -jnp.inf)) for gc_, gr_ in zip(g_col, g_row)]
        kb = [a * b for a, b in zip(kn, b_col)]
        vb = [a * b for a, b in zip(v, b_col)]
        kn16 = bf(kn)
        kk = [_dot_nt(a, b) for a, b in zip(bf(kb), kn16)]
        qk = [_dot_nt(a, b) for a, b in zip(bf(qn), kn16)]
        a_mat = [jnp.where(m_strict, x * d, 0.0) for x, d in zip(kk, decay)]
        intra16 = bf([x * d for x, d in zip(qk, decay)])
        inv = [eye - a for a in a_mat]
        p16 = bf(a_mat)
        n = 2
        while n < c:
            p16 = bf([_dot(x, x) for x in p16])
            inv = [i + _dot(i16, x) for i, i16, x in zip(inv, bf(inv), p16)]
            n *= 2
        eg = [jnp.exp(x) for x in g_col]
        rhs = bf([jnp.concatenate([a, b * e], axis=1) for a, b, e in zip(vb, kb, eg)])
        uw = [_dot(i16, x) for i16, x in zip(bf(inv), rhs)]
        u = [x[:, :HEAD_DIM] for x in uw]
        w16 = bf([x[:, HEAD_DIM:] for x in uw])
        g_last1 = [jnp.broadcast_to(x[c - 1:c, :], (blk, blk)) for x in g_col]
        g_last2 = [jnp.broadcast_to(x[blk - 1:blk, :], (blk, blk)) for x in g_col]
        kdec16 = bf([k_ * jnp.exp(jnp.where(first, l1, l2) - gc_)
                     for k_, l1, l2, gc_ in zip(kn, g_last1, g_last2, g_col)])
        qg16 = bf([a * e for a, e in zip(qn, eg)])
        e1 = [jnp.exp(x) for x in g_last1]
        e2 = [jnp.exp(x) for x in g_last2]

        state = [state_ref[hd] for hd in heads]
        for sb in range(unroll):
            mine = lambda xs: xs[sb * nh:(sb + 1) * nh]
            u_, w_, q_, k_ = mine(u), mine(w16), mine(qg16), mine(kdec16)
            s016 = bf(state)
            vn1 = [a[0:c] - _dot(b[0:c], s_) for a, b, s_ in zip(u_, w_, s016)]
            oq1 = [_dot(a[0:c], s_) for a, s_ in zip(q_, s016)]
            s1 = [s_ * e_ + _dot_tn(a[0:c], x) for s_, e_, a, x in zip(state, mine(e1), k_, bf(vn1))]
            s116 = bf(s1)
            vn2 = [a[c:blk] - _dot(b[c:blk], s_) for a, b, s_ in zip(u_, w_, s116)]
            oq2 = [_dot(a[c:blk], s_) for a, s_ in zip(q_, s116)]
            state = [s_ * e_ + _dot_tn(a[c:blk], x) for s_, e_, a, x in zip(s1, mine(e2), k_, bf(vn2))]
            for hd in heads:
                vn = jnp.concatenate([vn1[hd], vn2[hd]], axis=0).astype(BF16)
                oacc_ref[rows[sb], sls[hd]] = (jnp.concatenate([oq1[hd], oq2[hd]], axis=0)
                                               + _dot(mine(intra16)[hd], vn))
        for hd in heads:
            state_ref[hd] = state[hd]
        return carry

    lax.fori_loop(0, T // (blk * unroll), block, 0)

    nw = nw_ref[...]
    for hd in range(nh):
        sl = slice(hd * HEAD_DIM, (hd + 1) * HEAD_DIM)
        o = oacc_ref[:, sl]
        zz = z_ref[0, :, sl].astype(F32)
        y = o * lax.rsqrt(jnp.mean(o * o, axis=-1, keepdims=True) + EPS) * nw
        o_ref[0, :, sl] = (y * (zz * _sigmoid(zz))).astype(o_ref.dtype)


def _gated_delta(proj3, small3, conv_w, a_log, dt_bias, norm_w):
    b, s, _ = proj3.shape
    hh = GDN_HEADS
    T = GDN_T
    tiles = T // LANES
    width = hh * HEAD_DIM
    gate_tiles = lambda g: g.reshape(b, s // LANES, LANES, hh).transpose(0, 1, 3, 2)
    b_t = gate_tiles(small3[:, :, QK_ROPE:QK_ROPE + hh])
    a_t = gate_tiles(small3[:, :, QK_ROPE + hh:QK_ROPE + 2 * hh])
    lane_bcast = lambda p: jnp.broadcast_to(p.reshape(hh, 1), (hh, LANES))
    col_spec = lambda j: pl.BlockSpec((1, T, width), lambda bi, ti: (bi, ti, j))
    gate_spec = pl.BlockSpec((1, tiles, hh, LANES), lambda bi, ti: (bi, ti, 0, 0))
    cw_spec = lambda j: pl.BlockSpec((GDN_CONV, width), lambda bi, ti: (0, j))
    head_spec = pl.BlockSpec((hh, LANES), lambda bi, ti: (0, 0))
    buf = pltpu.VMEM((T + SUBLANES, width), F32)
    full = pltpu.VMEM((T, width), F32)
    gate_buf = pltpu.VMEM((tiles * hh, LANES), F32)
    return pl.pallas_call(
        _gdn_kernel,
        grid=(b, s // T),
        in_specs=[head_spec, head_spec,
                  col_spec(0), col_spec(1), col_spec(2), col_spec(3),
                  gate_spec, gate_spec,
                  cw_spec(0), cw_spec(1), cw_spec(2),
                  pl.BlockSpec((1, HEAD_DIM), lambda bi, ti: (0, 0))],
        out_specs=pl.BlockSpec((1, T, width), lambda bi, ti: (bi, ti, 0)),
        out_shape=jax.ShapeDtypeStruct((b, s, width), BF16),
        scratch_shapes=[pltpu.VMEM((hh, HEAD_DIM, HEAD_DIM), F32), buf, buf, buf, full, full, full,
                        gate_buf, gate_buf, full],
        compiler_params=_cparams(("parallel", "arbitrary")),
        name="gated_delta",
    )(lane_bcast(a_log), lane_bcast(dt_bias), proj3, proj3, proj3, proj3, a_t, b_t,
      conv_w, conv_w, conv_w, norm_w.reshape(1, HEAD_DIM))


def _rms_rows(x, w):
    ms = jnp.mean(x * x, axis=-1, keepdims=True)
    return x * lax.rsqrt(ms + EPS) * w


def _qproj_kernel(c_ref, nw_ref, w_ref, cos_ref, sin_ref, o_ref):
    hq = _rms_rows(c_ref[0].astype(F32), nw_ref[...]).astype(BF16)
    res = _dot(hq, w_ref[...])
    cos = cos_ref[0]
    sin = sin_ref[0]
    lane = lax.broadcasted_iota(jnp.int32, cos.shape, 1)
    scale = SOFTMAX_SCALE * LOG2E
    for hd in range(MLA_HEADS):
        base = hd * QK_PAD
        nope = res[:, base:base + QK_NOPE]
        rp = res[:, base + QK_NOPE:base + QK_PAD]
        rot = rp * cos + _rotate_half_unsigned(rp, lane) * sin
        o_ref[0, hd, :, 0:QK_NOPE] = (nope * scale).astype(BF16)
        o_ref[0, hd, :, QK_NOPE:QK_PAD] = (rot * scale).astype(BF16)


def _kvproj_kernel(c_ref, small_ref, nw_ref, w_ref, cos_ref, sin_ref, k_ref, v_ref):
    hkv = _rms_rows(c_ref[0].astype(F32), nw_ref[...]).astype(BF16)
    res = _dot(hkv, w_ref[...])
    cos = cos_ref[0]
    sin = sin_ref[0]
    lane = lax.broadcasted_iota(jnp.int32, cos.shape, 1)
    kr = small_ref[0]
    kr16 = (kr * cos + _rotate_half_unsigned(kr, lane) * sin).astype(BF16)
    nk = MLA_HEADS * QK_NOPE
    ones = jnp.ones((res.shape[0], V_HEAD), BF16)
    for hd in range(MLA_HEADS):
        k_ref[0, hd, :, 0:QK_NOPE] = res[:, hd * QK_NOPE:(hd + 1) * QK_NOPE].astype(BF16)
        k_ref[0, hd, :, QK_NOPE:QK_PAD] = kr16
        v_ref[0, hd, :, 0:V_HEAD] = res[:, nk + hd * V_HEAD:nk + (hd + 1) * V_HEAD].astype(BF16)
        v_ref[0, hd, :, V_HEAD:2 * V_HEAD] = ones


def _mla_projections(proj3, small3, q_norm_w, w_uq, kv_norm_w, w_ukv, cos_tab, sin_tab):
    b, s, _ = proj3.shape
    hh = MLA_HEADS
    tm = MLA_TM
    main_cq = (3 * GDN_HEADS * HEAD_DIM + GDN_HEADS * HEAD_DIM) // Q_LORA
    main_ckv = main_cq + 1
    wq = w_uq.reshape(Q_LORA, hh, QK_NOPE + QK_ROPE)
    wq = jnp.concatenate([wq, jnp.zeros((Q_LORA, hh, QK_PAD - QK_NOPE - QK_ROPE), w_uq.dtype)], axis=2)
    wq = wq.reshape(Q_LORA, hh * QK_PAD).astype(BF16)
    wkv = w_ukv.reshape(KV_LORA, hh, QK_NOPE + V_HEAD)
    wkv = jnp.concatenate([wkv[:, :, :QK_NOPE].reshape(KV_LORA, hh * QK_NOPE),
                           wkv[:, :, QK_NOPE:].reshape(KV_LORA, hh * V_HEAD)], axis=1).astype(BF16)
    tab_spec = pl.BlockSpec((1, tm, LANES), lambda bi, ti: (bi, ti, 0))
    q = pl.pallas_call(
        _qproj_kernel,
        grid=(b, s // tm),
        in_specs=[pl.BlockSpec((1, tm, Q_LORA), lambda bi, ti: (bi, ti, main_cq)),
                  pl.BlockSpec((1, Q_LORA), lambda bi, ti: (0, 0)),
                  pl.BlockSpec((Q_LORA, hh * QK_PAD), lambda bi, ti: (0, 0)),
                  tab_spec, tab_spec],
        out_specs=pl.BlockSpec((1, hh, tm, QK_PAD), lambda bi, ti: (bi, 0, ti, 0)),
        out_shape=jax.ShapeDtypeStruct((b, hh, s, QK_PAD), BF16),
        compiler_params=_cparams(("parallel", "parallel")),
        name="mla_q_proj",
    )(proj3, q_norm_w.reshape(1, Q_LORA), wq, cos_tab, sin_tab)
    k, v = pl.pallas_call(
        _kvproj_kernel,
        grid=(b, s // tm),
        in_specs=[pl.BlockSpec((1, tm, KV_LORA), lambda bi, ti: (bi, ti, main_ckv)),
                  tab_spec,
                  pl.BlockSpec((1, KV_LORA), lambda bi, ti: (0, 0)),
                  pl.BlockSpec((KV_LORA, hh * (QK_NOPE + V_HEAD)), lambda bi, ti: (0, 0)),
                  tab_spec, tab_spec],
        out_specs=[pl.BlockSpec((1, hh, tm, QK_PAD), lambda bi, ti: (bi, 0, ti, 0)),
                   pl.BlockSpec((1, hh, tm, 2 * V_HEAD), lambda bi, ti: (bi, 0, ti, 0))],
        out_shape=[jax.ShapeDtypeStruct((b, hh, s, QK_PAD), BF16),
                   jax.ShapeDtypeStruct((b, hh, s, 2 * V_HEAD), BF16)],
        compiler_params=_cparams(("parallel", "parallel")),
        name="mla_kv_proj",
    )(proj3, small3, kv_norm_w.reshape(1, KV_LORA), wkv, cos_tab, sin_tab)
    return q, k, v


def _attn_kernel(q_ref, k_ref, v_ref, nw_ref, o_ref, *scratch):
    nh = ATT_HEADS
    sa_refs, sb_refs, (m_ref, acc_ref) = scratch[:nh], scratch[nh:2 * nh], scratch[2 * nh:]
    heads = range(nh)
    i = pl.program_id(2)
    tq = q_ref.shape[2]
    tk = tq
    q = [q_ref[0, hd] for hd in heads]
    m_ref[...] = jnp.full(m_ref.shape, -jnp.inf, F32)
    acc_ref[...] = jnp.zeros(acc_ref.shape, F32)

    def scores(hd, j):
        kb = k_ref[0, hd, pl.ds(pl.multiple_of(j * tk, tk), tk), :]
        return _dot_nt(q[hd], kb)

    def update(hd, sc, j):
        vb = v_ref[0, hd, pl.ds(pl.multiple_of(j * tk, tk), tk), :]
        m_prev = m_ref[hd]
        m_new = jnp.maximum(m_prev, jnp.max(sc, axis=-1, keepdims=True))
        p = jnp.exp2(sc - jnp.concatenate([m_new] * (tk // LANES), axis=1))
        alpha = jnp.exp2(m_prev - m_new)
        acc_ref[hd] = jnp.concatenate([alpha, alpha], axis=1) * acc_ref[hd] + _dot(p.astype(BF16), vb)
        m_ref[hd] = m_new

    def update_diagonal(hd, sc):
        qpos = lax.broadcasted_iota(jnp.int32, sc.shape, 0)
        kpos = lax.broadcasted_iota(jnp.int32, sc.shape, 1)
        update(hd, jnp.where(kpos <= qpos, sc, -jnp.inf), i)

    for hd in heads:
        sa_refs[hd][...] = scores(hd, 0)

    def body(jj, carry):
        j = 2 * jj
        for hd in heads:
            sb_refs[hd][...] = scores(hd, j + 1)
        for hd in heads:
            update(hd, sa_refs[hd][...], j)
        for hd in heads:
            sa_refs[hd][...] = scores(hd, j + 2)
        for hd in heads:
            update(hd, sb_refs[hd][...], j + 1)
        return carry

    lax.fori_loop(0, i // 2, body, 0)

    @pl.when(i % 2 == 0)
    def _():
        for hd in heads:
            update_diagonal(hd, sa_refs[hd][...])

    @pl.when(i % 2 == 1)
    def _():
        for hd in heads:
            sb_refs[hd][...] = scores(hd, i)
        for hd in heads:
            update(hd, sa_refs[hd][...], i - 1)
        for hd in heads:
            update_diagonal(hd, sb_refs[hd][...])

    nw = nw_ref[...]
    for hd in heads:
        acc = acc_ref[hd]
        o = acc[:, :V_HEAD] / acc[:, V_HEAD:]
        y = o * lax.rsqrt(jnp.mean(o * o, axis=-1, keepdims=True) + EPS) * nw
        o_ref[0, :, hd * V_HEAD:(hd + 1) * V_HEAD] = y.astype(o_ref.dtype)


def _attention(q, k, v1, norm_w):
    b, hh, s, _ = q.shape
    tq = ATT_TQ
    nh = ATT_HEADS
    score_buf = pltpu.VMEM((tq, tq), F32)
    return pl.pallas_call(
        _attn_kernel,
        grid=(b, hh // nh, s // tq),
        in_specs=[pl.BlockSpec((1, nh, tq, QK_PAD), lambda bi, hi, qi: (bi, hi, qi, 0)),
                  pl.BlockSpec((1, nh, s, QK_PAD), lambda bi, hi, qi: (bi, hi, 0, 0)),
                  pl.BlockSpec((1, nh, s, 2 * V_HEAD), lambda bi, hi, qi: (bi, hi, 0, 0)),
                  pl.BlockSpec((1, V_HEAD), lambda bi, hi, qi: (0, 0))],
        out_specs=pl.BlockSpec((1, tq, nh * V_HEAD), lambda bi, hi, qi: (bi, qi, hi)),
        out_shape=jax.ShapeDtypeStruct((b, s, hh * V_HEAD), BF16),
        scratch_shapes=[score_buf] * (2 * nh) + [pltpu.VMEM((nh, tq, V_HEAD), F32),
                                                  pltpu.VMEM((nh, tq, 2 * V_HEAD), F32)],
        compiler_params=_cparams(("parallel", "parallel", "arbitrary")),
        name="mla_attention",
    )(q, k, v1, norm_w.reshape(1, V_HEAD))


def _outproj_kernel(og_ref, om_ref, x_ref, wg_ref, wm_ref, o_ref):
    o_ref[...] = x_ref[...] + _dot(og_ref[...], wg_ref[...]) + _dot(om_ref[...], wm_ref[...])


def _out_projection(o_gdn, o_mla, x2, w_out):
    m, d = x2.shape
    kg = o_gdn.shape[1]
    km = o_mla.shape[1]
    tm = OUT_TM
    w16 = w_out.astype(BF16)
    return pl.pallas_call(
        _outproj_kernel,
        grid=(m // tm,),
        in_specs=[pl.BlockSpec((tm, kg), lambda i: (i, 0)),
                  pl.BlockSpec((tm, km), lambda i: (i, 0)),
                  pl.BlockSpec((tm, d), lambda i: (i, 0)),
                  pl.BlockSpec((kg, d), lambda i: (0, 0)),
                  pl.BlockSpec((km, d), lambda i: (0, 0))],
        out_specs=pl.BlockSpec((tm, d), lambda i: (i, 0)),
        out_shape=jax.ShapeDtypeStruct((m, d), F32),
        compiler_params=_cparams(("parallel",)),
        name="out_projection",
    )(o_gdn, o_mla, x2, w16[:kg], w16[kg:])


def _ffn_kernel(x_ref, nw_ref, wg_ref, wu_ref, wd_ref, fw_ref, o_ref, h_ref, *, nj):
    j = pl.program_id(1)

    tm = x_ref.shape[0]

    @pl.when(j == 0)
    def _():
        x = x_ref[...]
        h_ref[...] = _rms_rows(x, nw_ref[...]).astype(BF16)
        o_ref[...] = x

    halves = [slice(0, tm // 2), slice(tm // 2, tm)]
    wg = wg_ref[...]
    wu = wu_ref[...]
    gu = [(_dot(h_ref[rows, :], wg), _dot(h_ref[rows, :], wu)) for rows in halves]
    act = [(g * _sigmoid(g) * u).astype(BF16) for g, u in gu]
    wd = wd_ref[...]
    for rows, a in zip(halves, act):
        o_ref[rows, :] += _dot(a, wd)

    @pl.when(j == nj - 1)
    def _():
        o_ref[...] = _rms_rows(o_ref[...], fw_ref[...])


def _ffn(x1, norm_w, w_gate, w_up, w_down, final_w):
    m, d = x1.shape
    dff = w_gate.shape[1]
    tm = FFN_TM
    tf = FFN_TF
    nj = dff // tf
    kern = functools.partial(_ffn_kernel, nj=nj)
    return pl.pallas_call(
        kern,
        grid=(m // tm, nj),
        in_specs=[pl.BlockSpec((tm, d), lambda i, j: (i, 0), pipeline_mode=pl.Buffered(1)),
                  pl.BlockSpec((1, d), lambda i, j: (0, 0)),
                  pl.BlockSpec((d, tf), lambda i, j: (0, j)),
                  pl.BlockSpec((d, tf), lambda i, j: (0, j)),
                  pl.BlockSpec((tf, d), lambda i, j: (j, 0)),
                  pl.BlockSpec((1, d), lambda i, j: (0, 0))],
        out_specs=pl.BlockSpec((tm, d), lambda i, j: (i, 0)),
        out_shape=jax.ShapeDtypeStruct((m, d), F32),
        scratch_shapes=[pltpu.VMEM((tm, d), BF16)],
        compiler_params=_cparams(("parallel", "arbitrary")),
        name="swiglu_ffn",
    )(x1, norm_w.reshape(1, d), w_gate.astype(BF16), w_up.astype(BF16), w_down.astype(BF16),
      final_w.reshape(1, d))


def kernel(x, positions, attn_norm_w, w_in, conv_w, a_log, dt_bias, gdn_norm_w, q_norm_w, w_uq,
           kv_norm_w, w_ukv, mla_out_norm_w, w_out, ffn_norm_w, w_gate, w_up, w_down, final_norm_w):
    b, s, d = x.shape
    assert w_in.shape[0] == 1, "the final rmsnorm is fused into the (single) layer's FFN kernel"
    l = 0
    cos_tab, sin_tab = _rope_tables(positions)
    x2 = x.reshape(b * s, d)
    proj, small = _in_projection(x2, attn_norm_w[l], w_in[l])
    proj3 = proj.reshape(b, s, proj.shape[1])
    small3 = small.reshape(b, s, LANES)
    o_gdn = _gated_delta(proj3, small3, conv_w[l], a_log[l], dt_bias[l], gdn_norm_w[l])
    q, k, v = _mla_projections(proj3, small3, q_norm_w[l], w_uq[l], kv_norm_w[l], w_ukv[l],
                               cos_tab, sin_tab)
    o_mla = _attention(q, k, v, mla_out_norm_w[l])
    x1 = _out_projection(o_gdn.reshape(b * s, -1), o_mla.reshape(b * s, -1), x2, w_out[l])
    out = _ffn(x1, ffn_norm_w[l], w_gate[l], w_up[l], w_down[l], final_norm_w)
    return out.reshape(b, s, d)
```

```python
import functools
import math

import jax
import jax.numpy as jnp
from jax import lax
from jax.experimental import pallas as pl
from jax.experimental.pallas import tpu as pltpu

F32 = jnp.float32
BF16 = jnp.bfloat16

GDN_HEADS = 8
HEAD_DIM = 128
GDN_CONV = 4
GDN_CHUNK = 64
MLA_HEADS = 8
QK_NOPE = 128
QK_ROPE = 64
V_HEAD = 128
Q_LORA = 512
KV_LORA = 512
ROPE_THETA = 10000.0
EPS = 1e-6

LANES = 128
SUBLANES = 8
VMEM_LIMIT_BYTES = 56 * 2**20

INPROJ_TM = 1024
INPROJ_TN = 1792
GDN_T = 512
GDN_BLK = 2 * GDN_CHUNK
GDN_UNROLL = 2
MLA_TM = 1024
ATT_TK = 512
ATT_HEADS = 2
OUT_TM = 512
FFN_TM = 1024
FFN_TF = 512

QK_PAD = 256
SOFTMAX_SCALE = (QK_NOPE + QK_ROPE) ** -0.5
LOG2E = math.log2(math.e)


def _cparams(semantics):
    return pltpu.CompilerParams(dimension_semantics=semantics, vmem_limit_bytes=VMEM_LIMIT_BYTES)


def _sigmoid(x):
    return 1.0 / (1.0 + jnp.exp(-x))


def _softplus(x):
    return jnp.maximum(x, 0.0) + jnp.log1p(jnp.exp(-jnp.abs(x)))


def _dot(a, b):
    return jnp.dot(a, b, preferred_element_type=F32)


def _dot_nt(a, b):
    return lax.dot_general(a, b, (((1,), (1,)), ((), ())), preferred_element_type=F32)


def _dot_tn(a, b):
    return lax.dot_general(a, b, (((0,), (0,)), ((), ())), preferred_element_type=F32)


def _rope_kernel(pos_ref, invf_ref, cos_ref, sin_ref):
    half = QK_ROPE // 2
    per_row = LANES // half
    rows = pos_ref.shape[0]
    ang = pos_ref[...].astype(F32) * invf_ref[...]
    cos = jnp.cos(ang)
    sin = jnp.sin(ang)
    lane = lax.broadcasted_iota(jnp.int32, ang.shape, 1)
    for g in range(per_row):
        to_front = lambda x: pltpu.roll(x, LANES - g * half, axis=1) if g else x
        c1 = to_front(cos)
        s1 = to_front(sin)
        c2 = pltpu.roll(c1, half, axis=1)
        s2 = pltpu.roll(s1, half, axis=1)
        token_rows = pl.ds(g, rows, stride=per_row)
        cos_ref[token_rows, :] = jnp.where(lane < half, c1, jnp.where(lane < QK_ROPE, c2, 0.0))
        sin_ref[token_rows, :] = jnp.where(lane < half, -s1, jnp.where(lane < QK_ROPE, s2, 0.0))


def _rope_tables(positions):
    b, s = positions.shape
    m = b * s
    half = QK_ROPE // 2
    per_row = LANES // half
    rows = m // per_row
    blk = min(rows, 1024)
    inv_freq = ROPE_THETA ** (-jnp.arange(half, dtype=F32) / half)
    invf = jnp.tile(inv_freq, per_row).reshape(1, LANES)
    pos_rep = jnp.broadcast_to(positions.reshape(rows, per_row, 1), (rows, per_row, half)).reshape(rows, LANES)
    cos_tab, sin_tab = pl.pallas_call(
        _rope_kernel,
        grid=(rows // blk,),
        in_specs=[pl.BlockSpec((blk, LANES), lambda i: (i, 0)),
                  pl.BlockSpec((1, LANES), lambda i: (0, 0))],
        out_specs=[pl.BlockSpec((blk * per_row, LANES), lambda i: (i, 0)),
                   pl.BlockSpec((blk * per_row, LANES), lambda i: (i, 0))],
        out_shape=[jax.ShapeDtypeStruct((m, LANES), F32)] * 2,
        compiler_params=_cparams(("parallel",)),
        name="rope_table",
    )(pos_rep, invf)
    return cos_tab.reshape(b, s, LANES), sin_tab.reshape(b, s, LANES)


def _rotate_half_unsigned(x):
    return pltpu.roll(x, QK_ROPE // 2, axis=1)


def _with_x2_duplicate(w_rope, gap):
    return jnp.concatenate([w_rope, gap, w_rope[..., QK_ROPE // 2:]], axis=-1)


def _inproj_kernel(x_ref, nw_ref, w_ref, o_ref, small_ref, gates_ref, h_ref, *, nj, small_off):
    j = pl.program_id(1)

    @pl.when(j == 0)
    def _():
        x = x_ref[...]
        ms = jnp.mean(x * x, axis=-1, keepdims=True)
        h_ref[...] = (x * lax.rsqrt(ms + EPS) * nw_ref[...]).astype(BF16)

    acc = _dot(h_ref[...], w_ref[...])
    o_ref[...] = acc.astype(BF16)

    @pl.when(j == nj - 1)
    def _():
        small = acc[:, small_off:small_off + LANES]
        small_ref[...] = small
        for r in range(gates_ref.shape[0]):
            tile = small[r * LANES:(r + 1) * LANES, :].T
            gates_ref[r] = tile[QK_ROPE:QK_ROPE + 2 * GDN_HEADS, :]


def _in_projection(x2, norm_w, w_in):
    m, d = x2.shape
    conv_ch = 3 * GDN_HEADS * HEAD_DIM
    gdn_v = GDN_HEADS * HEAD_DIM
    o = 0
    qkv = w_in[:, o:o + conv_ch]; o += conv_ch
    z = w_in[:, o:o + gdn_v]; o += gdn_v
    b_raw = w_in[:, o:o + GDN_HEADS]; o += GDN_HEADS
    a_raw = w_in[:, o:o + GDN_HEADS]; o += GDN_HEADS
    cq = w_in[:, o:o + Q_LORA]; o += Q_LORA
    ckv = w_in[:, o:o + KV_LORA]; o += KV_LORA
    kr = w_in[:, o:o + QK_ROPE]
    main_w = conv_ch + gdn_v + Q_LORA + KV_LORA
    tn = INPROJ_TN
    total = pl.cdiv(main_w + LANES, tn) * tn
    gates_w = jnp.concatenate([b_raw, a_raw], axis=1)
    gap = jnp.zeros((d, LANES - QK_ROPE - 2 * GDN_HEADS - QK_ROPE // 2), w_in.dtype)
    small_w = _with_x2_duplicate(kr, jnp.concatenate([gates_w, gap], axis=1))
    pad = jnp.zeros((d, total - main_w - LANES), w_in.dtype)
    w = jnp.concatenate([qkv, z, cq, ckv, small_w, pad], axis=1).astype(BF16)
    nj = total // tn
    small_off = main_w - (nj - 1) * tn
    tm = INPROJ_TM
    kern = functools.partial(_inproj_kernel, nj=nj, small_off=small_off)
    return pl.pallas_call(
        kern,
        grid=(m // tm, nj),
        in_specs=[pl.BlockSpec((tm, d), lambda i, j: (i, 0)),
                  pl.BlockSpec((1, d), lambda i, j: (0, 0)),
                  pl.BlockSpec((d, tn), lambda i, j: (0, j))],
        out_specs=[pl.BlockSpec((tm, tn), lambda i, j: (i, j)),
                   pl.BlockSpec((tm, LANES), lambda i, j: (i, 0)),
                   pl.BlockSpec((tm // LANES, 2 * GDN_HEADS, LANES), lambda i, j: (i, 0, 0))],
        out_shape=[jax.ShapeDtypeStruct((m, total), BF16),
                   jax.ShapeDtypeStruct((m, LANES), F32),
                   jax.ShapeDtypeStruct((m // LANES, 2 * GDN_HEADS, LANES), F32)],
        scratch_shapes=[pltpu.VMEM((tm, d), BF16)],
        compiler_params=_cparams(("parallel", "arbitrary")),
        name="in_projection",
    )(x2, norm_w.reshape(1, d), w)


def _gdn_kernel(alog_ref, dtb_ref, q_ref, k_ref, v_ref, z_ref, a_ref, b_ref,
                cwq_ref, cwk_ref, cwv_ref, nw_ref, o_ref,
                state_ref, xq_ref, xk_ref, xv_ref, qn_ref, kn_ref, vv_ref, gc_ref, beta_ref, oacc_ref):
    t = pl.program_id(1)
    T = q_ref.shape[1]
    nh = GDN_HEADS
    hist = SUBLANES
    blk = GDN_BLK
    c = GDN_CHUNK

    @pl.when(t == 0)
    def _():
        state_ref[...] = jnp.zeros_like(state_ref)
        zero_hist = jnp.zeros((hist, nh * HEAD_DIM), F32)
        xq_ref[0:hist, :] = zero_hist
        xk_ref[0:hist, :] = zero_hist
        xv_ref[0:hist, :] = zero_hist

    def conv_silu(in_ref, xbuf, cw_ref):
        xbuf[hist:hist + T, :] = in_ref[0].astype(F32)
        w = cw_ref[...]
        y = xbuf[hist - 3:hist - 3 + T, :] * w[0:1, :]
        for i in range(1, GDN_CONV):
            y = y + xbuf[hist - 3 + i:hist - 3 + i + T, :] * w[i:i + 1, :]
        xbuf[0:hist, :] = xbuf[T:T + hist, :]
        return y * _sigmoid(y)

    q = conv_silu(q_ref, xq_ref, cwq_ref)
    k = conv_silu(k_ref, xk_ref, cwk_ref)
    vv_ref[...] = conv_silu(v_ref, xv_ref, cwv_ref)
    for hd in range(nh):
        sl = slice(hd * HEAD_DIM, (hd + 1) * HEAD_DIM)
        qh = q[:, sl]
        kh = k[:, sl]
        qn_ref[:, sl] = qh * lax.rsqrt(jnp.sum(qh * qh, axis=-1, keepdims=True) + EPS) * (HEAD_DIM ** -0.5)
        kn_ref[:, sl] = kh * lax.rsqrt(jnp.sum(kh * kh, axis=-1, keepdims=True) + EPS)

    neg_a = -jnp.exp(alog_ref[...])
    dtb = dtb_ref[...]
    lane = lax.broadcasted_iota(jnp.int32, (nh, LANES), 1)
    in_chunk = lane % c
    for r in range(T // LANES):
        tile = pl.ds(r * nh, nh)
        beta_ref[tile, :] = _sigmoid(b_ref[0, r])
        gc = neg_a * _softplus(a_ref[0, r] + dtb)
        s = 1
        while s < c:
            gc = gc + jnp.where(in_chunk >= s, pltpu.roll(gc, s, axis=1), 0.0)
            s *= 2
        gc_ref[tile, :] = gc

    row = lax.broadcasted_iota(jnp.int32, (blk, blk), 0)
    col = lax.broadcasted_iota(jnp.int32, (blk, blk), 1)
    same_chunk = (row // c) == (col // c)
    m_tril = jnp.logical_and(same_chunk, row >= col)
    m_strict = jnp.logical_and(same_chunk, row > col)
    eye = jnp.where(row == col, 1.0, 0.0).astype(F32)
    first = row < c

    heads = range(nh)
    sls = [slice(hd * HEAD_DIM, (hd + 1) * HEAD_DIM) for hd in heads]
    bf = lambda xs: [x.astype(BF16) for x in xs]

    unroll = GDN_UNROLL
    streams = [(sb, hd) for sb in range(unroll) for hd in heads]

    def block(rr, carry):
        rows = [pl.ds(pl.multiple_of((rr * unroll + sb) * blk, blk), blk) for sb in range(unroll)]
        gate = lambda sb, hd: pl.ds((rr * unroll + sb) * nh + hd, 1)
        qn = [qn_ref[rows[sb], sls[hd]] for sb, hd in streams]
        kn = [kn_ref[rows[sb], sls[hd]] for sb, hd in streams]
        v = [vv_ref[rows[sb], sls[hd]] for sb, hd in streams]
        g_row = [jnp.broadcast_to(gc_ref[gate(sb, hd), :], (blk, blk)) for sb, hd in streams]
        g_col = [x.T for x in g_row]
        b_col = [jnp.broadcast_to(beta_ref[gate(sb, hd), :], (blk, blk)).T for sb, hd in streams]
        decay = [jnp.exp(jnp.where(m_tril, gc_ - gr_, -jnp.inf)) for gc_, gr_ in zip(g_col, g_row)]
        kb = [a * b for a, b in zip(kn, b_col)]
        vb = [a * b for a, b in zip(v, b_col)]
        kn16 = bf(kn)
        kk = [_dot_nt(a, b) for a, b in zip(bf(kb), kn16)]
        qk = [_dot_nt(a, b) for a, b in zip(bf(qn), kn16)]
        a_mat = [jnp.where(m_strict, x * d, 0.0) for x, d in zip(kk, decay)]
        intra16 = bf([x * d for x, d in zip(qk, decay)])
        inv = [eye - a for a in a_mat]
        p16 = bf(a_mat)
        n = 2
        while n < c:
            p16 = bf([_dot(x, x) for x in p16])
            inv = [i + _dot(i16, x) for i, i16, x in zip(inv, bf(inv), p16)]
            n *= 2
        eg = [jnp.exp(x) for x in g_col]
        rhs = bf([jnp.concatenate([a, b * e], axis=1) for a, b, e in zip(vb, kb, eg)])
        uw = [_dot(i16, x) for i16, x in zip(bf(inv), rhs)]
        u = [x[:, :HEAD_DIM] for x in uw]
        w16 = bf([x[:, HEAD_DIM:] for x in uw])
        g_last1 = [jnp.broadcast_to(x[c - 1:c, :], (blk, blk)) for x in g_col]
        g_last2 = [jnp.broadcast_to(x[blk - 1:blk, :], (blk, blk)) for x in g_col]
        kdec16 = bf([k_ * jnp.exp(jnp.where(first, l1, l2) - gc_)
                     for k_, l1, l2, gc_ in zip(kn, g_last1, g_last2, g_col)])
        qg16 = bf([a * e for a, e in zip(qn, eg)])
        e1 = [jnp.exp(x) for x in g_last1]
        e2 = [jnp.exp(x) for x in g_last2]

        state = [state_ref[hd] for hd in heads]
        for sb in range(unroll):
            mine = lambda xs: xs[sb * nh:(sb + 1) * nh]
            u_, w_, q_, k_ = mine(u), mine(w16), mine(qg16), mine(kdec16)
            s016 = bf(state)
            vn1 = [a[0:c] - _dot(b[0:c], s_) for a, b, s_ in zip(u_, w_, s016)]
            oq1 = [_dot(a[0:c], s_) for a, s_ in zip(q_, s016)]
            s1 = [s_ * e_ + _dot_tn(a[0:c], x) for s_, e_, a, x in zip(state, mine(e1), k_, bf(vn1))]
            s116 = bf(s1)
            vn2 = [a[c:blk] - _dot(b[c:blk], s_) for a, b, s_ in zip(u_, w_, s116)]
            oq2 = [_dot(a[c:blk], s_) for a, s_ in zip(q_, s116)]
            state = [s_ * e_ + _dot_tn(a[c:blk], x) for s_, e_, a, x in zip(s1, mine(e2), k_, bf(vn2))]
            for hd in heads:
                vn = jnp.concatenate([vn1[hd], vn2[hd]], axis=0).astype(BF16)
                oacc_ref[rows[sb], sls[hd]] = (jnp.concatenate([oq1[hd], oq2[hd]], axis=0)
                                               + _dot(mine(intra16)[hd], vn))
        for hd in heads:
            state_ref[hd] = state[hd]
        return carry

    lax.fori_loop(0, T // (blk * unroll), block, 0)

    nw = nw_ref[...]
    for hd in range(nh):
        sl = slice(hd * HEAD_DIM, (hd + 1) * HEAD_DIM)
        o = oacc_ref[:, sl]
        zz = z_ref[0, :, sl].astype(F32)
        y = o * lax.rsqrt(jnp.mean(o * o, axis=-1, keepdims=True) + EPS) * nw
        o_ref[0, :, sl] = (y * (zz * _sigmoid(zz))).astype(o_ref.dtype)


def _gated_delta(proj3, gates, conv_w, a_log, dt_bias, norm_w):
    b, s, _ = proj3.shape
    hh = GDN_HEADS
    T = GDN_T
    tiles = T // LANES
    width = hh * HEAD_DIM
    gates4 = gates.reshape(b, s // LANES, 2 * hh, LANES)
    lane_bcast = lambda p: jnp.broadcast_to(p.reshape(hh, 1), (hh, LANES))
    col_spec = lambda j: pl.BlockSpec((1, T, width), lambda bi, ti: (bi, ti, j))
    gate_spec = lambda j: pl.BlockSpec((1, tiles, hh, LANES), lambda bi, ti: (bi, ti, j, 0))
    cw_spec = lambda j: pl.BlockSpec((GDN_CONV, width), lambda bi, ti: (0, j))
    head_spec = pl.BlockSpec((hh, LANES), lambda bi, ti: (0, 0))
    buf = pltpu.VMEM((T + SUBLANES, width), F32)
    full = pltpu.VMEM((T, width), F32)
    gate_buf = pltpu.VMEM((tiles * hh, LANES), F32)
    return pl.pallas_call(
        _gdn_kernel,
        grid=(b, s // T),
        in_specs=[head_spec, head_spec,
                  col_spec(0), col_spec(1), col_spec(2), col_spec(3),
                  gate_spec(1), gate_spec(0),
                  cw_spec(0), cw_spec(1), cw_spec(2),
                  pl.BlockSpec((1, HEAD_DIM), lambda bi, ti: (0, 0))],
        out_specs=pl.BlockSpec((1, T, width), lambda bi, ti: (bi, ti, 0)),
        out_shape=jax.ShapeDtypeStruct((b, s, width), BF16),
        scratch_shapes=[pltpu.VMEM((hh, HEAD_DIM, HEAD_DIM), F32), buf, buf, buf, full, full, full,
                        gate_buf, gate_buf, full],
        compiler_params=_cparams(("parallel", "arbitrary")),
        name="gated_delta",
    )(lane_bcast(a_log), lane_bcast(dt_bias), proj3, proj3, proj3, proj3, gates4, gates4,
      conv_w, conv_w, conv_w, norm_w.reshape(1, HEAD_DIM))


def _rms_rows(x, w):
    ms = jnp.mean(x * x, axis=-1, keepdims=True)
    return x * lax.rsqrt(ms + EPS) * w


def _qproj_kernel(c_ref, nw_ref, w_ref, cos_ref, sin_ref, o_ref):
    hq = _rms_rows(c_ref[0].astype(F32), nw_ref[...]).astype(BF16)
    res = _dot(hq, w_ref[...])
    cos = cos_ref[0]
    sin = sin_ref[0]
    scale = SOFTMAX_SCALE * LOG2E
    for hd in range(MLA_HEADS):
        base = hd * QK_PAD
        nope = res[:, base:base + QK_NOPE]
        rp = res[:, base + QK_NOPE:base + QK_PAD]
        rot = rp * cos + _rotate_half_unsigned(rp) * sin
        o_ref[0, hd, :, 0:QK_NOPE] = (nope * scale).astype(BF16)
        o_ref[0, hd, :, QK_NOPE:QK_PAD] = (rot * scale).astype(BF16)


def _kvproj_kernel(c_ref, small_ref, nw_ref, w_ref, cos_ref, sin_ref, k_ref, v_ref):
    hkv = _rms_rows(c_ref[0].astype(F32), nw_ref[...]).astype(BF16)
    res = _dot(hkv, w_ref[...])
    cos = cos_ref[0]
    sin = sin_ref[0]
    kr = small_ref[0]
    kr16 = (kr * cos + _rotate_half_unsigned(kr) * sin).astype(BF16)
    nk = MLA_HEADS * QK_NOPE
    for hd in range(MLA_HEADS):
        k_ref[0, hd, :, 0:QK_NOPE] = res[:, hd * QK_NOPE:(hd + 1) * QK_NOPE].astype(BF16)
        k_ref[0, hd, :, QK_NOPE:QK_PAD] = kr16
        v_ref[0, hd] = res[:, nk + hd * V_HEAD:nk + (hd + 1) * V_HEAD].astype(BF16)


def _mla_projections(proj3, small3, q_norm_w, w_uq, kv_norm_w, w_ukv, cos_tab, sin_tab):
    b, s, _ = proj3.shape
    hh = MLA_HEADS
    tm = MLA_TM
    main_cq = (3 * GDN_HEADS * HEAD_DIM + GDN_HEADS * HEAD_DIM) // Q_LORA
    main_ckv = main_cq + 1
    wq = w_uq.reshape(Q_LORA, hh, QK_NOPE + QK_ROPE)
    gap = jnp.zeros((Q_LORA, hh, QK_PAD - QK_NOPE - QK_ROPE - QK_ROPE // 2), w_uq.dtype)
    wq = jnp.concatenate([wq[:, :, :QK_NOPE], _with_x2_duplicate(wq[:, :, QK_NOPE:], gap)], axis=2)
    wq = wq.reshape(Q_LORA, hh * QK_PAD).astype(BF16)
    wkv = w_ukv.reshape(KV_LORA, hh, QK_NOPE + V_HEAD)
    wkv = jnp.concatenate([wkv[:, :, :QK_NOPE].reshape(KV_LORA, hh * QK_NOPE),
                           wkv[:, :, QK_NOPE:].reshape(KV_LORA, hh * V_HEAD)], axis=1).astype(BF16)
    tab_spec = pl.BlockSpec((1, tm, LANES), lambda bi, ti: (bi, ti, 0))
    q = pl.pallas_call(
        _qproj_kernel,
        grid=(b, s // tm),
        in_specs=[pl.BlockSpec((1, tm, Q_LORA), lambda bi, ti: (bi, ti, main_cq)),
                  pl.BlockSpec((1, Q_LORA), lambda bi, ti: (0, 0)),
                  pl.BlockSpec((Q_LORA, hh * QK_PAD), lambda bi, ti: (0, 0)),
                  tab_spec, tab_spec],
        out_specs=pl.BlockSpec((1, hh, tm, QK_PAD), lambda bi, ti: (bi, 0, ti, 0)),
        out_shape=jax.ShapeDtypeStruct((b, hh, s, QK_PAD), BF16),
        compiler_params=_cparams(("parallel", "parallel")),
        name="mla_q_proj",
    )(proj3, q_norm_w.reshape(1, Q_LORA), wq, cos_tab, sin_tab)
    k, v = pl.pallas_call(
        _kvproj_kernel,
        grid=(b, s // tm),
        in_specs=[pl.BlockSpec((1, tm, KV_LORA), lambda bi, ti: (bi, ti, main_ckv)),
                  tab_spec,
                  pl.BlockSpec((1, KV_LORA), lambda bi, ti: (0, 0)),
                  pl.BlockSpec((KV_LORA, hh * (QK_NOPE + V_HEAD)), lambda bi, ti: (0, 0)),
                  tab_spec, tab_spec],
        out_specs=[pl.BlockSpec((1, hh, tm, QK_PAD), lambda bi, ti: (bi, 0, ti, 0)),
                   pl.BlockSpec((1, hh, tm, V_HEAD), lambda bi, ti: (bi, 0, ti, 0))],
        out_shape=[jax.ShapeDtypeStruct((b, hh, s, QK_PAD), BF16),
                   jax.ShapeDtypeStruct((b, hh, s, V_HEAD), BF16)],
        compiler_params=_cparams(("parallel", "parallel")),
        name="mla_kv_proj",
    )(proj3, small3, kv_norm_w.reshape(1, KV_LORA), wkv, cos_tab, sin_tab)
    return q, k, v


def _attn_kernel(q_ref, k_ref, v_ref, nw_ref, o_ref, *scratch):
    nh = ATT_HEADS
    tk = ATT_TK
    streams = [(hd, half) for hd in range(nh) for half in range(2)]
    ns = len(streams)
    sa_refs, sb_refs, (m_ref, acc_ref) = scratch[:ns], scratch[ns:2 * ns], scratch[2 * ns:]
    i = pl.program_id(2)
    q = [q_ref[0, hd, half * tk:(half + 1) * tk, :] for hd, half in streams]
    m_ref[...] = jnp.full(m_ref.shape, -jnp.inf, F32)
    acc_ref[...] = jnp.zeros(acc_ref.shape, F32)
    ones = jnp.ones((tk, V_HEAD), BF16)

    def scores(st, j):
        kb = k_ref[0, streams[st][0], pl.ds(pl.multiple_of(j * tk, tk), tk), :]
        return _dot_nt(q[st], kb)

    def update(st, sc, j, diagonal=False):
        if diagonal:
            qpos = lax.broadcasted_iota(jnp.int32, sc.shape, 0)
            kpos = lax.broadcasted_iota(jnp.int32, sc.shape, 1)
            sc = jnp.where(kpos <= qpos, sc, -jnp.inf)
        vb = v_ref[0, streams[st][0], pl.ds(pl.multiple_of(j * tk, tk), tk), :]
        vb = jnp.concatenate([vb, ones], axis=1)
        m_prev = m_ref[st]
        m_new = jnp.maximum(m_prev, jnp.max(sc, axis=-1, keepdims=True))
        p = jnp.exp2(sc - jnp.concatenate([m_new] * (tk // LANES), axis=1))
        alpha = jnp.exp2(m_prev - m_new)
        acc_ref[st] = jnp.concatenate([alpha, alpha], axis=1) * acc_ref[st] + _dot(p.astype(BF16), vb)
        m_ref[st] = m_new

    every = range(ns)
    for st in every:
        sa_refs[st][...] = scores(st, 0)

    def body(jj, carry):
        j = 2 * jj
        for st in every:
            sb_refs[st][...] = scores(st, j + 1)
        for st in every:
            update(st, sa_refs[st][...], j)
        for st in every:
            sa_refs[st][...] = scores(st, j + 2)
        for st in every:
            update(st, sb_refs[st][...], j + 1)
        return carry

    lax.fori_loop(0, i, body, 0)

    lower = [st for st in every if streams[st][1] == 1]
    for st in lower:
        sb_refs[st][...] = scores(st, 2 * i + 1)
    for st in every:
        update(st, sa_refs[st][...], 2 * i, diagonal=streams[st][1] == 0)
    for st in lower:
        update(st, sb_refs[st][...], 2 * i + 1, diagonal=True)

    nw = nw_ref[...]
    for st, (hd, half) in enumerate(streams):
        acc = acc_ref[st]
        o = acc[:, :V_HEAD] / acc[:, V_HEAD:]
        y = o * lax.rsqrt(jnp.mean(o * o, axis=-1, keepdims=True) + EPS) * nw
        o_ref[0, half * tk:(half + 1) * tk, hd * V_HEAD:(hd + 1) * V_HEAD] = y.astype(o_ref.dtype)


def _attention(q, k, v, norm_w):
    b, hh, s, _ = q.shape
    tk = ATT_TK
    tq = 2 * tk
    nh = ATT_HEADS
    ns = 2 * nh
    score_buf = pltpu.VMEM((tk, tk), F32)
    return pl.pallas_call(
        _attn_kernel,
        grid=(b, hh // nh, s // tq),
        in_specs=[pl.BlockSpec((1, nh, tq, QK_PAD), lambda bi, hi, qi: (bi, hi, qi, 0)),
                  pl.BlockSpec((1, nh, s, QK_PAD), lambda bi, hi, qi: (bi, hi, 0, 0)),
                  pl.BlockSpec((1, nh, s, V_HEAD), lambda bi, hi, qi: (bi, hi, 0, 0)),
                  pl.BlockSpec((1, V_HEAD), lambda bi, hi, qi: (0, 0))],
        out_specs=pl.BlockSpec((1, tq, nh * V_HEAD), lambda bi, hi, qi: (bi, qi, hi)),
        out_shape=jax.ShapeDtypeStruct((b, s, hh * V_HEAD), BF16),
        scratch_shapes=[score_buf] * (2 * ns) + [pltpu.VMEM((ns, tk, V_HEAD), F32),
                                                  pltpu.VMEM((ns, tk, 2 * V_HEAD), F32)],
        compiler_params=_cparams(("parallel", "parallel", "arbitrary")),
        name="mla_attention",
    )(q, k, v, norm_w.reshape(1, V_HEAD))


def _outproj_kernel(og_ref, om_ref, x_ref, wg_ref, wm_ref, o_ref):
    o_ref[...] = x_ref[...] + _dot(og_ref[...], wg_ref[...]) + _dot(om_ref[...], wm_ref[...])


def _out_projection(o_gdn, o_mla, x2, w_out):
    m, d = x2.shape
    kg = o_gdn.shape[1]
    km = o_mla.shape[1]
    tm = OUT_TM
    w16 = w_out.astype(BF16)
    return pl.pallas_call(
        _outproj_kernel,
        grid=(m // tm,),
        in_specs=[pl.BlockSpec((tm, kg), lambda i: (i, 0)),
                  pl.BlockSpec((tm, km), lambda i: (i, 0)),
                  pl.BlockSpec((tm, d), lambda i: (i, 0)),
                  pl.BlockSpec((kg, d), lambda i: (0, 0)),
                  pl.BlockSpec((km, d), lambda i: (0, 0))],
        out_specs=pl.BlockSpec((tm, d), lambda i: (i, 0)),
        out_shape=jax.ShapeDtypeStruct((m, d), F32),
        compiler_params=_cparams(("parallel",)),
        name="out_projection",
    )(o_gdn, o_mla, x2, w16[:kg], w16[kg:])


def _ffn_kernel(x_ref, nw_ref, wg_ref, wu_ref, wd_ref, fw_ref, o_ref, h_ref, *, nj):
    j = pl.program_id(1)

    tm = x_ref.shape[0]

    @pl.when(j == 0)
    def _():
        x = x_ref[...]
        h_ref[...] = _rms_rows(x, nw_ref[...]).astype(BF16)
        o_ref[...] = x

    halves = [slice(0, tm // 2), slice(tm // 2, tm)]
    wg = wg_ref[...]
    wu = wu_ref[...]
    gu = [(_dot(h_ref[rows, :], wg), _dot(h_ref[rows, :], wu)) for rows in halves]
    act = [(g * _sigmoid(g) * u).astype(BF16) for g, u in gu]
    wd = wd_ref[...]
    for rows, a in zip(halves, act):
        o_ref[rows, :] += _dot(a, wd)

    @pl.when(j == nj - 1)
    def _():
        o_ref[...] = _rms_rows(o_ref[...], fw_ref[...])


def _ffn(x1, norm_w, w_gate, w_up, w_down, final_w):
    m, d = x1.shape
    dff = w_gate.shape[1]
    tm = FFN_TM
    tf = FFN_TF
    nj = dff // tf
    kern = functools.partial(_ffn_kernel, nj=nj)
    return pl.pallas_call(
        kern,
        grid=(m // tm, nj),
        in_specs=[pl.BlockSpec((tm, d), lambda i, j: (i, 0)),
                  pl.BlockSpec((1, d), lambda i, j: (0, 0)),
                  pl.BlockSpec((d, tf), lambda i, j: (0, j)),
                  pl.BlockSpec((d, tf), lambda i, j: (0, j)),
                  pl.BlockSpec((tf, d), lambda i, j: (j, 0)),
                  pl.BlockSpec((1, d), lambda i, j: (0, 0))],
        out_specs=pl.BlockSpec((tm, d), lambda i, j: (i, 0)),
        out_shape=jax.ShapeDtypeStruct((m, d), F32),
        scratch_shapes=[pltpu.VMEM((tm, d), BF16)],
        compiler_params=_cparams(("parallel", "arbitrary")),
        name="swiglu_ffn",
    )(x1, norm_w.reshape(1, d), w_gate.astype(BF16), w_up.astype(BF16), w_down.astype(BF16),
      final_w.reshape(1, d))


def kernel(x, positions, attn_norm_w, w_in, conv_w, a_log, dt_bias, gdn_norm_w, q_norm_w, w_uq,
           kv_norm_w, w_ukv, mla_out_norm_w, w_out, ffn_norm_w, w_gate, w_up, w_down, final_norm_w):
    b, s, d = x.shape
    assert w_in.shape[0] == 1, "the final rmsnorm is fused into the (single) layer's FFN kernel"
    l = 0
    cos_tab, sin_tab = _rope_tables(positions)
    x2 = x.reshape(b * s, d)
    proj, small, gates = _in_projection(x2, attn_norm_w[l], w_in[l])
    proj3 = proj.reshape(b, s, proj.shape[1])
    small3 = small.reshape(b, s, LANES)
    o_gdn = _gated_delta(proj3, gates, conv_w[l], a_log[l], dt_bias[l], gdn_norm_w[l])
    q, k, v = _mla_projections(proj3, small3, q_norm_w[l], w_uq[l], kv_norm_w[l], w_ukv[l],
                               cos_tab, sin_tab)
    o_mla = _attention(q, k, v, mla_out_norm_w[l])
    x1 = _out_projection(o_gdn.reshape(b * s, -1), o_mla.reshape(b * s, -1), x2, w_out[l])
    out = _ffn(x1, ffn_norm_w[l], w_gate[l], w_up[l], w_down[l], final_norm_w)
    return out.reshape(b, s, d)
```

```python
import functools
import math

import jax
import jax.numpy as jnp
from jax import lax
from jax.experimental import pallas as pl
from jax.experimental.pallas import tpu as pltpu

F32 = jnp.float32
BF16 = jnp.bfloat16

GDN_HEADS = 8
HEAD_DIM = 128
GDN_CONV = 4
GDN_CHUNK = 64
MLA_HEADS = 8
QK_NOPE = 128
QK_ROPE = 64
V_HEAD = 128
Q_LORA = 512
KV_LORA = 512
ROPE_THETA = 10000.0
EPS = 1e-6

LANES = 128
SUBLANES = 8
VMEM_LIMIT_BYTES = 56 * 2**20

INPROJ_TM = 1024
INPROJ_TN = 1792
GDN_T = 512
GDN_BLK = 2 * GDN_CHUNK
GDN_UNROLL = 2
MLA_TM = 1024
ATT_TK = 512
ATT_HEADS = 2
OUT_TM = 512
FFN_TM = 1024
FFN_TF = 512

QK_PAD = 256
SOFTMAX_SCALE = (QK_NOPE + QK_ROPE) ** -0.5
LOG2E = math.log2(math.e)


def _cparams(semantics):
    return pltpu.CompilerParams(dimension_semantics=semantics, vmem_limit_bytes=VMEM_LIMIT_BYTES)


def _sigmoid(x):
    return 1.0 / (1.0 + jnp.exp(-x))


def _softplus(x):
    return jnp.maximum(x, 0.0) + jnp.log1p(jnp.exp(-jnp.abs(x)))


def _dot(a, b):
    return jnp.dot(a, b, preferred_element_type=F32)


def _dot_nt(a, b):
    return lax.dot_general(a, b, (((1,), (1,)), ((), ())), preferred_element_type=F32)


def _dot_tn(a, b):
    return lax.dot_general(a, b, (((0,), (0,)), ((), ())), preferred_element_type=F32)


def _rope_kernel(pos_ref, invf_ref, cos_ref, sin_ref):
    half = QK_ROPE // 2
    per_row = LANES // half
    rows = pos_ref.shape[0]
    ang = pos_ref[...].astype(F32) * invf_ref[...]
    cos = jnp.cos(ang)
    sin = jnp.sin(ang)
    lane = lax.broadcasted_iota(jnp.int32, ang.shape, 1)
    for g in range(per_row):
        to_front = lambda x: pltpu.roll(x, LANES - g * half, axis=1) if g else x
        c1 = to_front(cos)
        s1 = to_front(sin)
        c2 = pltpu.roll(c1, half, axis=1)
        s2 = pltpu.roll(s1, half, axis=1)
        token_rows = pl.ds(g, rows, stride=per_row)
        cos_ref[token_rows, :] = jnp.where(lane < half, c1, jnp.where(lane < QK_ROPE, c2, 0.0))
        sin_ref[token_rows, :] = jnp.where(lane < half, -s1, jnp.where(lane < QK_ROPE, s2, 0.0))


def _rope_tables(positions):
    b, s = positions.shape
    m = b * s
    half = QK_ROPE // 2
    per_row = LANES // half
    rows = m // per_row
    blk = min(rows, 1024)
    inv_freq = ROPE_THETA ** (-jnp.arange(half, dtype=F32) / half)
    invf = jnp.tile(inv_freq, per_row).reshape(1, LANES)
    pos_rep = jnp.broadcast_to(positions.reshape(rows, per_row, 1), (rows, per_row, half)).reshape(rows, LANES)
    cos_tab, sin_tab = pl.pallas_call(
        _rope_kernel,
        grid=(rows // blk,),
        in_specs=[pl.BlockSpec((blk, LANES), lambda i: (i, 0)),
                  pl.BlockSpec((1, LANES), lambda i: (0, 0))],
        out_specs=[pl.BlockSpec((blk * per_row, LANES), lambda i: (i, 0)),
                   pl.BlockSpec((blk * per_row, LANES), lambda i: (i, 0))],
        out_shape=[jax.ShapeDtypeStruct((m, LANES), F32)] * 2,
        compiler_params=_cparams(("parallel",)),
        name="rope_table",
    )(pos_rep, invf)
    return cos_tab.reshape(b, s, LANES), sin_tab.reshape(b, s, LANES)


def _rotate_half_unsigned(x):
    return pltpu.roll(x, QK_ROPE // 2, axis=1)


def _with_x2_duplicate(w_rope, gap):
    return jnp.concatenate([w_rope, gap, w_rope[..., QK_ROPE // 2:]], axis=-1)


def _inproj_kernel(x_ref, nw_ref, w_ref, o_ref, small_ref, gates_ref, h_ref, *, nj, small_off):
    j = pl.program_id(1)

    @pl.when(j == 0)
    def _():
        x = x_ref[...]
        ms = jnp.mean(x * x, axis=-1, keepdims=True)
        h_ref[...] = (x * lax.rsqrt(ms + EPS) * nw_ref[...]).astype(BF16)

    acc = _dot(h_ref[...], w_ref[...])
    o_ref[...] = acc.astype(BF16)

    @pl.when(j == nj - 1)
    def _():
        small = acc[:, small_off:small_off + LANES]
        small_ref[...] = small
        for r in range(gates_ref.shape[0]):
            tile = small[r * LANES:(r + 1) * LANES, :].T
            gates_ref[r] = tile[QK_ROPE:QK_ROPE + 2 * GDN_HEADS, :]


def _in_projection(x2, norm_w, w_in):
    m, d = x2.shape
    conv_ch = 3 * GDN_HEADS * HEAD_DIM
    gdn_v = GDN_HEADS * HEAD_DIM
    w_in = w_in.astype(BF16)
    o = 0
    qkv = w_in[:, o:o + conv_ch]; o += conv_ch
    z = w_in[:, o:o + gdn_v]; o += gdn_v
    b_raw = w_in[:, o:o + GDN_HEADS]; o += GDN_HEADS
    a_raw = w_in[:, o:o + GDN_HEADS]; o += GDN_HEADS
    cq = w_in[:, o:o + Q_LORA]; o += Q_LORA
    ckv = w_in[:, o:o + KV_LORA]; o += KV_LORA
    kr = w_in[:, o:o + QK_ROPE]
    main_w = conv_ch + gdn_v + Q_LORA + KV_LORA
    tn = INPROJ_TN
    total = pl.cdiv(main_w + LANES, tn) * tn
    gates_w = jnp.concatenate([b_raw, a_raw], axis=1)
    gap = jnp.zeros((d, LANES - QK_ROPE - 2 * GDN_HEADS - QK_ROPE // 2), w_in.dtype)
    small_w = _with_x2_duplicate(kr, jnp.concatenate([gates_w, gap], axis=1))
    pad = jnp.zeros((d, total - main_w - LANES), w_in.dtype)
    w = jnp.concatenate([qkv, z, cq, ckv, small_w, pad], axis=1)
    nj = total // tn
    small_off = main_w - (nj - 1) * tn
    tm = INPROJ_TM
    kern = functools.partial(_inproj_kernel, nj=nj, small_off=small_off)
    return pl.pallas_call(
        kern,
        grid=(m // tm, nj),
        in_specs=[pl.BlockSpec((tm, d), lambda i, j: (i, 0)),
                  pl.BlockSpec((1, d), lambda i, j: (0, 0)),
                  pl.BlockSpec((d, tn), lambda i, j: (0, j))],
        out_specs=[pl.BlockSpec((tm, tn), lambda i, j: (i, j)),
                   pl.BlockSpec((tm, LANES), lambda i, j: (i, 0)),
                   pl.BlockSpec((tm // LANES, 2 * GDN_HEADS, LANES), lambda i, j: (i, 0, 0))],
        out_shape=[jax.ShapeDtypeStruct((m, total), BF16),
                   jax.ShapeDtypeStruct((m, LANES), F32),
                   jax.ShapeDtypeStruct((m // LANES, 2 * GDN_HEADS, LANES), F32)],
        scratch_shapes=[pltpu.VMEM((tm, d), BF16)],
        compiler_params=_cparams(("parallel", "arbitrary")),
        name="in_projection",
    )(x2, norm_w.reshape(1, d), w)


def _gdn_kernel(alog_ref, dtb_ref, q_ref, k_ref, v_ref, z_ref, a_ref, b_ref,
                cwq_ref, cwk_ref, cwv_ref, nw_ref, o_ref,
                state_ref, xq_ref, xk_ref, xv_ref, qn_ref, kn_ref, vv_ref, gc_ref, beta_ref, oacc_ref):
    t = pl.program_id(1)
    T = q_ref.shape[1]
    nh = GDN_HEADS
    blk = GDN_BLK
    hist = blk
    c = GDN_CHUNK

    @pl.when(t == 0)
    def _():
        state_ref[...] = jnp.zeros_like(state_ref)
        zero_hist = jnp.zeros((hist, nh * HEAD_DIM), BF16)
        xq_ref[0:hist, :] = zero_hist
        xk_ref[0:hist, :] = zero_hist
        xv_ref[0:hist, :] = zero_hist

    taps = GDN_CONV - 1
    sel_row = lax.broadcasted_iota(jnp.int32, (taps * blk, hist + blk), 0)
    sel_col = lax.broadcasted_iota(jnp.int32, (taps * blk, hist + blk), 1)
    shift_sel = jnp.where(sel_col == sel_row % blk + hist - taps + sel_row // blk, 1.0, 0.0).astype(BF16)

    def conv_silu(xbuf, cw_ref, m):
        window = xbuf[m * blk:m * blk + hist + blk, :]
        w = cw_ref[...]
        shifted = _dot(shift_sel, window)
        y = window[hist:, :].astype(F32) * w[taps:taps + 1, :]
        for i in range(taps):
            y = y + shifted[i * blk:(i + 1) * blk, :] * w[i:i + 1, :]
        return y * _sigmoid(y)

    for in_ref, xbuf in ((q_ref, xq_ref), (k_ref, xk_ref), (v_ref, xv_ref)):
        xbuf[hist:hist + T, :] = in_ref[0]
    for m in range(T // blk):
        rows = slice(m * blk, (m + 1) * blk)
        q = conv_silu(xq_ref, cwq_ref, m)
        k = conv_silu(xk_ref, cwk_ref, m)
        vv_ref[rows, :] = conv_silu(xv_ref, cwv_ref, m)
        for hd in range(nh):
            sl = slice(hd * HEAD_DIM, (hd + 1) * HEAD_DIM)
            qh = q[:, sl]
            kh = k[:, sl]
            qn_ref[rows, sl] = qh * lax.rsqrt(jnp.sum(qh * qh, axis=-1, keepdims=True) + EPS) * (HEAD_DIM ** -0.5)
            kn_ref[rows, sl] = kh * lax.rsqrt(jnp.sum(kh * kh, axis=-1, keepdims=True) + EPS)
    for xbuf in (xq_ref, xk_ref, xv_ref):
        xbuf[0:hist, :] = xbuf[T:T + hist, :]

    neg_a = -jnp.exp(alog_ref[...])
    dtb = dtb_ref[...]
    lane = lax.broadcasted_iota(jnp.int32, (nh, LANES), 1)
    in_chunk = lane % c
    for r in range(T // LANES):
        tile = pl.ds(r * nh, nh)
        beta_ref[tile, :] = _sigmoid(b_ref[0, r])
        gc = neg_a * _softplus(a_ref[0, r] + dtb)
        s = 1
        while s < c:
            gc = gc + jnp.where(in_chunk >= s, pltpu.roll(gc, s, axis=1), 0.0)
            s *= 2
        gc_ref[tile, :] = gc

    row = lax.broadcasted_iota(jnp.int32, (blk, blk), 0)
    col = lax.broadcasted_iota(jnp.int32, (blk, blk), 1)
    same_chunk = (row // c) == (col // c)
    m_tril = jnp.logical_and(same_chunk, row >= col)
    m_strict = jnp.logical_and(same_chunk, row > col)
    eye = jnp.where(row == col, 1.0, 0.0).astype(F32)
    first = row < c

    heads = range(nh)
    sls = [slice(hd * HEAD_DIM, (hd + 1) * HEAD_DIM) for hd in heads]
    bf = lambda xs: [x.astype(BF16) for x in xs]

    unroll = GDN_UNROLL
    streams = [(sb, hd) for sb in range(unroll) for hd in heads]

    def block(rr, carry):
        rows = [pl.ds(pl.multiple_of((rr * unroll + sb) * blk, blk), blk) for sb in range(unroll)]
        gate = lambda sb, hd: pl.ds((rr * unroll + sb) * nh + hd, 1)
        qn = [qn_ref[rows[sb], sls[hd]] for sb, hd in streams]
        kn = [kn_ref[rows[sb], sls[hd]] for sb, hd in streams]
        v = [vv_ref[rows[sb], sls[hd]] for sb, hd in streams]
        g_row = [jnp.broadcast_to(gc_ref[gate(sb, hd), :], (blk, blk)) for sb, hd in streams]
        g_col = [x.T for x in g_row]
        b_col = [jnp.broadcast_to(beta_ref[gate(sb, hd), :], (blk, blk)).T for sb, hd in streams]
        decay = [jnp.exp(jnp.where(m_tril, gc_ - gr_, -jnp.inf)) for gc_, gr_ in zip(g_col, g_row)]
        kb = [a * b for a, b in zip(kn, b_col)]
        vb = [a * b for a, b in zip(v, b_col)]
        kn16 = bf(kn)
        kk = [_dot_nt(a, b) for a, b in zip(bf(kb), kn16)]
        qk = [_dot_nt(a, b) for a, b in zip(bf(qn), kn16)]
        a_mat = [jnp.where(m_strict, x * d, 0.0) for x, d in zip(kk, decay)]
        intra16 = bf([x * d for x, d in zip(qk, decay)])
        inv = [eye - a for a in a_mat]
        p16 = bf(a_mat)
        n = 2
        while n < c:
            p16 = bf([_dot(x, x) for x in p16])
            inv = [i + _dot(i16, x) for i, i16, x in zip(inv, bf(inv), p16)]
            n *= 2
        eg = [jnp.exp(x) for x in g_col]
        rhs = bf([jnp.concatenate([a, b * e], axis=1) for a, b, e in zip(vb, kb, eg)])
        uw = [_dot(i16, x) for i16, x in zip(bf(inv), rhs)]
        u = [x[:, :HEAD_DIM] for x in uw]
        w16 = bf([x[:, HEAD_DIM:] for x in uw])
        g_last1 = [jnp.broadcast_to(x[c - 1:c, :], (blk, blk)) for x in g_col]
        g_last2 = [jnp.broadcast_to(x[blk - 1:blk, :], (blk, blk)) for x in g_col]
        kdec16 = bf([k_ * jnp.exp(jnp.where(first, l1, l2) - gc_)
                     for k_, l1, l2, gc_ in zip(kn, g_last1, g_last2, g_col)])
        qg16 = bf([a * e for a, e in zip(qn, eg)])
        e1 = [jnp.exp(x) for x in g_last1]
        e2 = [jnp.exp(x) for x in g_last2]

        state = [state_ref[hd] for hd in heads]
        for sb in range(unroll):
            mine = lambda xs: xs[sb * nh:(sb + 1) * nh]
            u_, w_, q_, k_ = mine(u), mine(w16), mine(qg16), mine(kdec16)
            s016 = bf(state)
            vn1 = [a[0:c] - _dot(b[0:c], s_) for a, b, s_ in zip(u_, w_, s016)]
            oq1 = [_dot(a[0:c], s_) for a, s_ in zip(q_, s016)]
            s1 = [s_ * e_ + _dot_tn(a[0:c], x) for s_, e_, a, x in zip(state, mine(e1), k_, bf(vn1))]
            s116 = bf(s1)
            vn2 = [a[c:blk] - _dot(b[c:blk], s_) for a, b, s_ in zip(u_, w_, s116)]
            oq2 = [_dot(a[c:blk], s_) for a, s_ in zip(q_, s116)]
            state = [s_ * e_ + _dot_tn(a[c:blk], x) for s_, e_, a, x in zip(s1, mine(e2), k_, bf(vn2))]
            for hd in heads:
                vn = jnp.concatenate([vn1[hd], vn2[hd]], axis=0).astype(BF16)
                oacc_ref[rows[sb], sls[hd]] = (jnp.concatenate([oq1[hd], oq2[hd]], axis=0)
                                               + _dot(mine(intra16)[hd], vn))
        for hd in heads:
            state_ref[hd] = state[hd]
        return carry

    lax.fori_loop(0, T // (blk * unroll), block, 0)

    nw = nw_ref[...]
    for hd in range(nh):
        sl = slice(hd * HEAD_DIM, (hd + 1) * HEAD_DIM)
        o = oacc_ref[:, sl]
        zz = z_ref[0, :, sl].astype(F32)
        y = o * lax.rsqrt(jnp.mean(o * o, axis=-1, keepdims=True) + EPS) * nw
        o_ref[0, :, sl] = (y * (zz * _sigmoid(zz))).astype(o_ref.dtype)


def _gated_delta(proj3, gates, conv_w, a_log, dt_bias, norm_w):
    b, s, _ = proj3.shape
    hh = GDN_HEADS
    T = GDN_T
    tiles = T // LANES
    width = hh * HEAD_DIM
    gates4 = gates.reshape(b, s // LANES, 2 * hh, LANES)
    lane_bcast = lambda p: jnp.broadcast_to(p.reshape(hh, 1), (hh, LANES))
    col_spec = lambda j: pl.BlockSpec((1, T, width), lambda bi, ti: (bi, ti, j))
    gate_spec = lambda j: pl.BlockSpec((1, tiles, hh, LANES), lambda bi, ti: (bi, ti, j, 0))
    cw_spec = lambda j: pl.BlockSpec((GDN_CONV, width), lambda bi, ti: (0, j))
    head_spec = pl.BlockSpec((hh, LANES), lambda bi, ti: (0, 0))
    buf = pltpu.VMEM((T + GDN_BLK, width), BF16)
    full = pltpu.VMEM((T, width), F32)
    gate_buf = pltpu.VMEM((tiles * hh, LANES), F32)
    return pl.pallas_call(
        _gdn_kernel,
        grid=(b, s // T),
        in_specs=[head_spec, head_spec,
                  col_spec(0), col_spec(1), col_spec(2), col_spec(3),
                  gate_spec(1), gate_spec(0),
                  cw_spec(0), cw_spec(1), cw_spec(2),
                  pl.BlockSpec((1, HEAD_DIM), lambda bi, ti: (0, 0))],
        out_specs=pl.BlockSpec((1, T, width), lambda bi, ti: (bi, ti, 0)),
        out_shape=jax.ShapeDtypeStruct((b, s, width), BF16),
        scratch_shapes=[pltpu.VMEM((hh, HEAD_DIM, HEAD_DIM), F32), buf, buf, buf, full, full, full,
                        gate_buf, gate_buf, full],
        compiler_params=_cparams(("parallel", "arbitrary")),
        name="gated_delta",
    )(lane_bcast(a_log), lane_bcast(dt_bias), proj3, proj3, proj3, proj3, gates4, gates4,
      conv_w, conv_w, conv_w, norm_w.reshape(1, HEAD_DIM))


def _rms_rows(x, w):
    ms = jnp.mean(x * x, axis=-1, keepdims=True)
    return x * lax.rsqrt(ms + EPS) * w


def _qproj_kernel(c_ref, nw_ref, w_ref, cos_ref, sin_ref, o_ref):
    hq = _rms_rows(c_ref[0].astype(F32), nw_ref[...]).astype(BF16)
    res = _dot(hq, w_ref[...])
    cos = cos_ref[0]
    sin = sin_ref[0]
    scale = SOFTMAX_SCALE * LOG2E
    for hd in range(MLA_HEADS):
        base = hd * QK_PAD
        nope = res[:, base:base + QK_NOPE]
        rp = res[:, base + QK_NOPE:base + QK_PAD]
        rot = rp * cos + _rotate_half_unsigned(rp) * sin
        o_ref[0, hd, :, 0:QK_NOPE] = (nope * scale).astype(BF16)
        o_ref[0, hd, :, QK_NOPE:QK_PAD] = (rot * scale).astype(BF16)


def _kvproj_kernel(c_ref, small_ref, nw_ref, w_ref, cos_ref, sin_ref, k_ref, v_ref):
    hkv = _rms_rows(c_ref[0].astype(F32), nw_ref[...]).astype(BF16)
    res = _dot(hkv, w_ref[...])
    cos = cos_ref[0]
    sin = sin_ref[0]
    kr = small_ref[0]
    kr16 = (kr * cos + _rotate_half_unsigned(kr) * sin).astype(BF16)
    nk = MLA_HEADS * QK_NOPE
    for hd in range(MLA_HEADS):
        k_ref[0, hd, :, 0:QK_NOPE] = res[:, hd * QK_NOPE:(hd + 1) * QK_NOPE].astype(BF16)
        k_ref[0, hd, :, QK_NOPE:QK_PAD] = kr16
        v_ref[0, hd] = res[:, nk + hd * V_HEAD:nk + (hd + 1) * V_HEAD].astype(BF16)


def _mla_projections(proj3, small3, q_norm_w, w_uq, kv_norm_w, w_ukv, cos_tab, sin_tab):
    b, s, _ = proj3.shape
    hh = MLA_HEADS
    tm = MLA_TM
    main_cq = (3 * GDN_HEADS * HEAD_DIM + GDN_HEADS * HEAD_DIM) // Q_LORA
    main_ckv = main_cq + 1
    wq = w_uq.reshape(Q_LORA, hh, QK_NOPE + QK_ROPE)
    gap = jnp.zeros((Q_LORA, hh, QK_PAD - QK_NOPE - QK_ROPE - QK_ROPE // 2), w_uq.dtype)
    wq = jnp.concatenate([wq[:, :, :QK_NOPE], _with_x2_duplicate(wq[:, :, QK_NOPE:], gap)], axis=2)
    wq = wq.reshape(Q_LORA, hh * QK_PAD).astype(BF16)
    wkv = w_ukv.reshape(KV_LORA, hh, QK_NOPE + V_HEAD)
    wkv = jnp.concatenate([wkv[:, :, :QK_NOPE].reshape(KV_LORA, hh * QK_NOPE),
                           wkv[:, :, QK_NOPE:].reshape(KV_LORA, hh * V_HEAD)], axis=1).astype(BF16)
    tab_spec = pl.BlockSpec((1, tm, LANES), lambda bi, ti: (bi, ti, 0))
    q = pl.pallas_call(
        _qproj_kernel,
        grid=(b, s // tm),
        in_specs=[pl.BlockSpec((1, tm, Q_LORA), lambda bi, ti: (bi, ti, main_cq)),
                  pl.BlockSpec((1, Q_LORA), lambda bi, ti: (0, 0)),
                  pl.BlockSpec((Q_LORA, hh * QK_PAD), lambda bi, ti: (0, 0)),
                  tab_spec, tab_spec],
        out_specs=pl.BlockSpec((1, hh, tm, QK_PAD), lambda bi, ti: (bi, 0, ti, 0)),
        out_shape=jax.ShapeDtypeStruct((b, hh, s, QK_PAD), BF16),
        compiler_params=_cparams(("parallel", "parallel")),
        name="mla_q_proj",
    )(proj3, q_norm_w.reshape(1, Q_LORA), wq, cos_tab, sin_tab)
    k, v = pl.pallas_call(
        _kvproj_kernel,
        grid=(b, s // tm),
        in_specs=[pl.BlockSpec((1, tm, KV_LORA), lambda bi, ti: (bi, ti, main_ckv)),
                  tab_spec,
                  pl.BlockSpec((1, KV_LORA), lambda bi, ti: (0, 0)),
                  pl.BlockSpec((KV_LORA, hh * (QK_NOPE + V_HEAD)), lambda bi, ti: (0, 0)),
                  tab_spec, tab_spec],
        out_specs=[pl.BlockSpec((1, hh, tm, QK_PAD), lambda bi, ti: (bi, 0, ti, 0)),
                   pl.BlockSpec((1, hh, tm, V_HEAD), lambda bi, ti: (bi, 0, ti, 0))],
        out_shape=[jax.ShapeDtypeStruct((b, hh, s, QK_PAD), BF16),
                   jax.ShapeDtypeStruct((b, hh, s, V_HEAD), BF16)],
        compiler_params=_cparams(("parallel", "parallel")),
        name="mla_kv_proj",
    )(proj3, small3, kv_norm_w.reshape(1, KV_LORA), wkv, cos_tab, sin_tab)
    return q, k, v


def _attn_kernel(q_ref, k_ref, v_ref, nw_ref, o_ref, *scratch):
    nh = ATT_HEADS
    tk = ATT_TK
    streams = [(hd, half) for hd in range(nh) for half in range(2)]
    ns = len(streams)
    sa_refs, sb_refs, (m_ref, acc_ref) = scratch[:ns], scratch[ns:2 * ns], scratch[2 * ns:]
    i = pl.program_id(2)
    q = [q_ref[0, hd, half * tk:(half + 1) * tk, :] for hd, half in streams]
    m_ref[...] = jnp.full(m_ref.shape, -jnp.inf, F32)
    acc_ref[...] = jnp.zeros(acc_ref.shape, F32)
    ones = jnp.ones((tk, V_HEAD), BF16)

    def scores(st, j):
        kb = k_ref[0, streams[st][0], pl.ds(pl.multiple_of(j * tk, tk), tk), :]
        return _dot_nt(q[st], kb)

    def update(st, sc, j, diagonal=False):
        if diagonal:
            qpos = lax.broadcasted_iota(jnp.int32, sc.shape, 0)
            kpos = lax.broadcasted_iota(jnp.int32, sc.shape, 1)
            sc = jnp.where(kpos <= qpos, sc, -jnp.inf)
        vb = v_ref[0, streams[st][0], pl.ds(pl.multiple_of(j * tk, tk), tk), :]
        vb = jnp.concatenate([vb, ones], axis=1)
        m_prev = m_ref[st]
        m_new = jnp.maximum(m_prev, jnp.max(sc, axis=-1, keepdims=True))
        p = jnp.exp2(sc - jnp.concatenate([m_new] * (tk // LANES), axis=1))
        alpha = jnp.exp2(m_prev - m_new)
        acc_ref[st] = jnp.concatenate([alpha, alpha], axis=1) * acc_ref[st] + _dot(p.astype(BF16), vb)
        m_ref[st] = m_new

    every = range(ns)
    for st in every:
        sa_refs[st][...] = scores(st, 0)

    def body(jj, carry):
        j = 2 * jj
        for st in every:
            sb_refs[st][...] = scores(st, j + 1)
        for st in every:
            update(st, sa_refs[st][...], j)
        for st in every:
            sa_refs[st][...] = scores(st, j + 2)
        for st in every:
            update(st, sb_refs[st][...], j + 1)
        return carry

    lax.fori_loop(0, i, body, 0)

    lower = [st for st in every if streams[st][1] == 1]
    for st in lower:
        sb_refs[st][...] = scores(st, 2 * i + 1)
    for st in every:
        update(st, sa_refs[st][...], 2 * i, diagonal=streams[st][1] == 0)
    for st in lower:
        update(st, sb_refs[st][...], 2 * i + 1, diagonal=True)

    nw = nw_ref[...]
    for st, (hd, half) in enumerate(streams):
        acc = acc_ref[st]
        o = acc[:, :V_HEAD] / acc[:, V_HEAD:]
        y = o * lax.rsqrt(jnp.mean(o * o, axis=-1, keepdims=True) + EPS) * nw
        o_ref[0, half * tk:(half + 1) * tk, hd * V_HEAD:(hd + 1) * V_HEAD] = y.astype(o_ref.dtype)


def _attention(q, k, v, norm_w):
    b, hh, s, _ = q.shape
    tk = ATT_TK
    tq = 2 * tk
    nh = ATT_HEADS
    ns = 2 * nh
    score_buf = pltpu.VMEM((tk, tk), F32)
    return pl.pallas_call(
        _attn_kernel,
        grid=(b, hh // nh, s // tq),
        in_specs=[pl.BlockSpec((1, nh, tq, QK_PAD), lambda bi, hi, qi: (bi, hi, qi, 0)),
                  pl.BlockSpec((1, nh, s, QK_PAD), lambda bi, hi, qi: (bi, hi, 0, 0)),
                  pl.BlockSpec((1, nh, s, V_HEAD), lambda bi, hi, qi: (bi, hi, 0, 0)),
                  pl.BlockSpec((1, V_HEAD), lambda bi, hi, qi: (0, 0))],
        out_specs=pl.BlockSpec((1, tq, nh * V_HEAD), lambda bi, hi, qi: (bi, qi, hi)),
        out_shape=jax.ShapeDtypeStruct((b, s, hh * V_HEAD), BF16),
        scratch_shapes=[score_buf] * (2 * ns) + [pltpu.VMEM((ns, tk, V_HEAD), F32),
                                                  pltpu.VMEM((ns, tk, 2 * V_HEAD), F32)],
        compiler_params=_cparams(("parallel", "parallel", "arbitrary")),
        name="mla_attention",
    )(q, k, v, norm_w.reshape(1, V_HEAD))


def _outproj_kernel(og_ref, om_ref, x_ref, wg_ref, wm_ref, o_ref):
    o_ref[...] = x_ref[...] + _dot(og_ref[...], wg_ref[...]) + _dot(om_ref[...], wm_ref[...])


def _out_projection(o_gdn, o_mla, x2, w_out):
    m, d = x2.shape
    kg = o_gdn.shape[1]
    km = o_mla.shape[1]
    tm = OUT_TM
    w16 = w_out.astype(BF16)
    return pl.pallas_call(
        _outproj_kernel,
        grid=(m // tm,),
        in_specs=[pl.BlockSpec((tm, kg), lambda i: (i, 0)),
                  pl.BlockSpec((tm, km), lambda i: (i, 0)),
                  pl.BlockSpec((tm, d), lambda i: (i, 0)),
                  pl.BlockSpec((kg, d), lambda i: (0, 0)),
                  pl.BlockSpec((km, d), lambda i: (0, 0))],
        out_specs=pl.BlockSpec((tm, d), lambda i: (i, 0)),
        out_shape=jax.ShapeDtypeStruct((m, d), F32),
        compiler_params=_cparams(("parallel",)),
        name="out_projection",
    )(o_gdn, o_mla, x2, w16[:kg], w16[kg:])


def _ffn_kernel(x_ref, nw_ref, wg_ref, wu_ref, wd_ref, fw_ref, o_ref, h_ref, *, nj):
    j = pl.program_id(1)

    tm = x_ref.shape[0]

    @pl.when(j == 0)
    def _():
        x = x_ref[...]
        h_ref[...] = _rms_rows(x, nw_ref[...]).astype(BF16)
        o_ref[...] = x

    halves = [slice(0, tm // 2), slice(tm // 2, tm)]
    wg = wg_ref[...]
    wu = wu_ref[...]
    gu = [(_dot(h_ref[rows, :], wg), _dot(h_ref[rows, :], wu)) for rows in halves]
    act = [(g * _sigmoid(g) * u).astype(BF16) for g, u in gu]
    wd = wd_ref[...]
    for rows, a in zip(halves, act):
        o_ref[rows, :] += _dot(a, wd)

    @pl.when(j == nj - 1)
    def _():
        o_ref[...] = _rms_rows(o_ref[...], fw_ref[...])


def _ffn(x1, norm_w, w_gate, w_up, w_down, final_w):
    m, d = x1.shape
    dff = w_gate.shape[1]
    tm = FFN_TM
    tf = FFN_TF
    nj = dff // tf
    kern = functools.partial(_ffn_kernel, nj=nj)
    return pl.pallas_call(
        kern,
        grid=(m // tm, nj),
        in_specs=[pl.BlockSpec((tm, d), lambda i, j: (i, 0)),
                  pl.BlockSpec((1, d), lambda i, j: (0, 0)),
                  pl.BlockSpec((d, tf), lambda i, j: (0, j)),
                  pl.BlockSpec((d, tf), lambda i, j: (0, j)),
                  pl.BlockSpec((tf, d), lambda i, j: (j, 0)),
                  pl.BlockSpec((1, d), lambda i, j: (0, 0))],
        out_specs=pl.BlockSpec((tm, d), lambda i, j: (i, 0)),
        out_shape=jax.ShapeDtypeStruct((m, d), F32),
        scratch_shapes=[pltpu.VMEM((tm, d), BF16)],
        compiler_params=_cparams(("parallel", "arbitrary")),
        name="swiglu_ffn",
    )(x1, norm_w.reshape(1, d), w_gate.astype(BF16), w_up.astype(BF16), w_down.astype(BF16),
      final_w.reshape(1, d))


def kernel(x, positions, attn_norm_w, w_in, conv_w, a_log, dt_bias, gdn_norm_w, q_norm_w, w_uq,
           kv_norm_w, w_ukv, mla_out_norm_w, w_out, ffn_norm_w, w_gate, w_up, w_down, final_norm_w):
    b, s, d = x.shape
    assert w_in.shape[0] == 1, "the final rmsnorm is fused into the (single) layer's FFN kernel"
    l = 0
    cos_tab, sin_tab = _rope_tables(positions)
    x2 = x.reshape(b * s, d)
    proj, small, gates = _in_projection(x2, attn_norm_w[l], w_in[l])
    proj3 = proj.reshape(b, s, proj.shape[1])
    small3 = small.reshape(b, s, LANES)
    o_gdn = _gated_delta(proj3, gates, conv_w[l], a_log[l], dt_bias[l], gdn_norm_w[l])
    q, k, v = _mla_projections(proj3, small3, q_norm_w[l], w_uq[l], kv_norm_w[l], w_ukv[l],
                               cos_tab, sin_tab)
    o_mla = _attention(q, k, v, mla_out_norm_w[l])
    x1 = _out_projection(o_gdn.reshape(b * s, -1), o_mla.reshape(b * s, -1), x2, w_out[l])
    out = _ffn(x1, ffn_norm_w[l], w_gate[l], w_up[l], w_down[l], final_norm_w)
    return out.reshape(b, s, d)
```

```python
import functools
import math

import jax
import jax.numpy as jnp
from jax import lax
from jax.experimental import pallas as pl
from jax.experimental.pallas import tpu as pltpu

F32 = jnp.float32
BF16 = jnp.bfloat16

GDN_HEADS = 8
HEAD_DIM = 128
GDN_CONV = 4
GDN_CHUNK = 64
MLA_HEADS = 8
QK_NOPE = 128
QK_ROPE = 64
V_HEAD = 128
Q_LORA = 512
KV_LORA = 512
ROPE_THETA = 10000.0
EPS = 1e-6

LANES = 128
SUBLANES = 8
VMEM_LIMIT_BYTES = 56 * 2**20

INPROJ_TM = 1024
INPROJ_TN = 1792
GDN_T = 512
GDN_BLK = 2 * GDN_CHUNK
GDN_UNROLL = 2
MLA_TM = 1024
ATT_TK = 512
ATT_HEADS = 2
OUT_TM = 512
FFN_TM = 1024
FFN_TF = 512

QK_PAD = 256
SOFTMAX_SCALE = (QK_NOPE + QK_ROPE) ** -0.5
LOG2E = math.log2(math.e)


def _cparams(semantics):
    return pltpu.CompilerParams(dimension_semantics=semantics, vmem_limit_bytes=VMEM_LIMIT_BYTES)


def _sigmoid(x):
    return 1.0 / (1.0 + jnp.exp(-x))


def _softplus(x):
    return jnp.maximum(x, 0.0) + jnp.log1p(jnp.exp(-jnp.abs(x)))


def _dot(a, b):
    return jnp.dot(a, b, preferred_element_type=F32)


def _dot_nt(a, b):
    return lax.dot_general(a, b, (((1,), (1,)), ((), ())), preferred_element_type=F32)


def _dot_tn(a, b):
    return lax.dot_general(a, b, (((0,), (0,)), ((), ())), preferred_element_type=F32)


def _rope_kernel(pos_ref, invf_ref, cos_ref, sin_ref):
    half = QK_ROPE // 2
    per_row = LANES // half
    rows = pos_ref.shape[0]
    ang = pos_ref[...].astype(F32) * invf_ref[...]
    cos = jnp.cos(ang)
    sin = jnp.sin(ang)
    lane = lax.broadcasted_iota(jnp.int32, ang.shape, 1)
    for g in range(per_row):
        to_front = lambda x: pltpu.roll(x, LANES - g * half, axis=1) if g else x
        c1 = to_front(cos)
        s1 = to_front(sin)
        c2 = pltpu.roll(c1, half, axis=1)
        s2 = pltpu.roll(s1, half, axis=1)
        token_rows = pl.ds(g, rows, stride=per_row)
        cos_ref[token_rows, :] = jnp.where(lane < half, c1, jnp.where(lane < QK_ROPE, c2, 0.0))
        sin_ref[token_rows, :] = jnp.where(lane < half, -s1, jnp.where(lane < QK_ROPE, s2, 0.0))


def _rope_tables(positions):
    b, s = positions.shape
    m = b * s
    half = QK_ROPE // 2
    per_row = LANES // half
    rows = m // per_row
    blk = min(rows, 1024)
    inv_freq = ROPE_THETA ** (-jnp.arange(half, dtype=F32) / half)
    invf = jnp.tile(inv_freq, per_row).reshape(1, LANES)
    pos_rep = jnp.broadcast_to(positions.reshape(rows, per_row, 1), (rows, per_row, half)).reshape(rows, LANES)
    cos_tab, sin_tab = pl.pallas_call(
        _rope_kernel,
        grid=(rows // blk,),
        in_specs=[pl.BlockSpec((blk, LANES), lambda i: (i, 0)),
                  pl.BlockSpec((1, LANES), lambda i: (0, 0))],
        out_specs=[pl.BlockSpec((blk * per_row, LANES), lambda i: (i, 0)),
                   pl.BlockSpec((blk * per_row, LANES), lambda i: (i, 0))],
        out_shape=[jax.ShapeDtypeStruct((m, LANES), F32)] * 2,
        compiler_params=_cparams(("parallel",)),
        name="rope_table",
    )(pos_rep, invf)
    return cos_tab.reshape(b, s, LANES), sin_tab.reshape(b, s, LANES)


def _rotate_half_unsigned(x):
    return pltpu.roll(x, QK_ROPE // 2, axis=1)


def _with_x2_duplicate(w_rope, gap):
    return jnp.concatenate([w_rope, gap, w_rope[..., QK_ROPE // 2:]], axis=-1)


def _inproj_kernel(x_ref, nw_ref, w_ref, o_ref, small_ref, gates_ref, h_ref, *, nj, small_off):
    j = pl.program_id(1)

    @pl.when(j == 0)
    def _():
        x = x_ref[...]
        ms = jnp.mean(x * x, axis=-1, keepdims=True)
        h_ref[...] = (x * lax.rsqrt(ms + EPS) * nw_ref[...]).astype(BF16)

    acc = _dot(h_ref[...], w_ref[...])
    o_ref[...] = acc.astype(BF16)

    @pl.when(j == nj - 1)
    def _():
        small = acc[:, small_off:small_off + LANES]
        small_ref[...] = small
        for r in range(gates_ref.shape[0]):
            tile = small[r * LANES:(r + 1) * LANES, :].T
            gates_ref[r] = tile[QK_ROPE:QK_ROPE + 2 * GDN_HEADS, :]


def _in_projection(x2, norm_w, w_in):
    m, d = x2.shape
    conv_ch = 3 * GDN_HEADS * HEAD_DIM
    gdn_v = GDN_HEADS * HEAD_DIM
    w_in = w_in.astype(BF16)
    o = 0
    qkv = w_in[:, o:o + conv_ch]; o += conv_ch
    z = w_in[:, o:o + gdn_v]; o += gdn_v
    b_raw = w_in[:, o:o + GDN_HEADS]; o += GDN_HEADS
    a_raw = w_in[:, o:o + GDN_HEADS]; o += GDN_HEADS
    cq = w_in[:, o:o + Q_LORA]; o += Q_LORA
    ckv = w_in[:, o:o + KV_LORA]; o += KV_LORA
    kr = w_in[:, o:o + QK_ROPE]
    main_w = conv_ch + gdn_v + Q_LORA + KV_LORA
    tn = INPROJ_TN
    total = pl.cdiv(main_w + LANES, tn) * tn
    gates_w = jnp.concatenate([b_raw, a_raw], axis=1)
    gap = jnp.zeros((d, LANES - QK_ROPE - 2 * GDN_HEADS - QK_ROPE // 2), w_in.dtype)
    small_w = _with_x2_duplicate(kr, jnp.concatenate([gates_w, gap], axis=1))
    pad = jnp.zeros((d, total - main_w - LANES), w_in.dtype)
    w = jnp.concatenate([qkv, z, cq, ckv, small_w, pad], axis=1)
    nj = total // tn
    small_off = main_w - (nj - 1) * tn
    tm = INPROJ_TM
    kern = functools.partial(_inproj_kernel, nj=nj, small_off=small_off)
    return pl.pallas_call(
        kern,
        grid=(m // tm, nj),
        in_specs=[pl.BlockSpec((tm, d), lambda i, j: (i, 0)),
                  pl.BlockSpec((1, d), lambda i, j: (0, 0)),
                  pl.BlockSpec((d, tn), lambda i, j: (0, j))],
        out_specs=[pl.BlockSpec((tm, tn), lambda i, j: (i, j)),
                   pl.BlockSpec((tm, LANES), lambda i, j: (i, 0)),
                   pl.BlockSpec((tm // LANES, 2 * GDN_HEADS, LANES), lambda i, j: (i, 0, 0))],
        out_shape=[jax.ShapeDtypeStruct((m, total), BF16),
                   jax.ShapeDtypeStruct((m, LANES), F32),
                   jax.ShapeDtypeStruct((m // LANES, 2 * GDN_HEADS, LANES), F32)],
        scratch_shapes=[pltpu.VMEM((tm, d), BF16)],
        compiler_params=_cparams(("parallel", "arbitrary")),
        name="in_projection",
    )(x2, norm_w.reshape(1, d), w)


def _gdn_kernel(alog_ref, dtb_ref, q_ref, k_ref, v_ref, z_ref, a_ref, b_ref,
                cwq_ref, cwk_ref, cwv_ref, nw_ref, o_ref,
                state_ref, xq_ref, xk_ref, xv_ref, qn_ref, kn_ref, vv_ref, gc_ref, beta_ref, oacc_ref):
    t = pl.program_id(1)
    T = q_ref.shape[1]
    nh = GDN_HEADS
    blk = GDN_BLK
    hist = blk
    c = GDN_CHUNK

    @pl.when(t == 0)
    def _():
        state_ref[...] = jnp.zeros_like(state_ref)
        zero_hist = jnp.zeros((hist, nh * HEAD_DIM), BF16)
        xq_ref[0:hist, :] = zero_hist
        xk_ref[0:hist, :] = zero_hist
        xv_ref[0:hist, :] = zero_hist

    taps = GDN_CONV - 1
    sel_row = lax.broadcasted_iota(jnp.int32, (taps * blk, hist + blk), 0)
    sel_col = lax.broadcasted_iota(jnp.int32, (taps * blk, hist + blk), 1)
    shift_sel = jnp.where(sel_col == sel_row % blk + hist - taps + sel_row // blk, 1.0, 0.0).astype(BF16)

    def conv_silu(xbuf, cw_ref, m):
        window = xbuf[m * blk:m * blk + hist + blk, :]
        w = cw_ref[...]
        shifted = _dot(shift_sel, window)
        y = window[hist:, :].astype(F32) * w[taps:taps + 1, :]
        for i in range(taps):
            y = y + shifted[i * blk:(i + 1) * blk, :] * w[i:i + 1, :]
        return y * _sigmoid(y)

    for in_ref, xbuf in ((q_ref, xq_ref), (k_ref, xk_ref), (v_ref, xv_ref)):
        xbuf[hist:hist + T, :] = in_ref[0]
    for m in range(T // blk):
        rows = slice(m * blk, (m + 1) * blk)
        q = conv_silu(xq_ref, cwq_ref, m)
        k = conv_silu(xk_ref, cwk_ref, m)
        vv_ref[rows, :] = conv_silu(xv_ref, cwv_ref, m)
        for hd in range(nh):
            sl = slice(hd * HEAD_DIM, (hd + 1) * HEAD_DIM)
            qh = q[:, sl]
            kh = k[:, sl]
            qn_ref[rows, sl] = qh * lax.rsqrt(jnp.sum(qh * qh, axis=-1, keepdims=True) + EPS) * (HEAD_DIM ** -0.5)
            kn_ref[rows, sl] = kh * lax.rsqrt(jnp.sum(kh * kh, axis=-1, keepdims=True) + EPS)
    for xbuf in (xq_ref, xk_ref, xv_ref):
        xbuf[0:hist, :] = xbuf[T:T + hist, :]

    neg_a = -jnp.exp(alog_ref[...])
    dtb = dtb_ref[...]
    lane = lax.broadcasted_iota(jnp.int32, (nh, LANES), 1)
    in_chunk = lane % c
    for r in range(T // LANES):
        tile = pl.ds(r * nh, nh)
        beta_ref[tile, :] = _sigmoid(b_ref[0, r])
        gc = neg_a * _softplus(a_ref[0, r] + dtb)
        s = 1
        while s < c:
            gc = gc + jnp.where(in_chunk >= s, pltpu.roll(gc, s, axis=1), 0.0)
            s *= 2
        gc_ref[tile, :] = gc

    row = lax.broadcasted_iota(jnp.int32, (blk, blk), 0)
    col = lax.broadcasted_iota(jnp.int32, (blk, blk), 1)
    same_chunk = (row // c) == (col // c)
    m_tril = jnp.logical_and(same_chunk, row >= col)
    m_strict = jnp.logical_and(same_chunk, row > col)
    eye = jnp.where(row == col, 1.0, 0.0).astype(F32)
    first = row < c

    heads = range(nh)
    sls = [slice(hd * HEAD_DIM, (hd + 1) * HEAD_DIM) for hd in heads]
    bf = lambda xs: [x.astype(BF16) for x in xs]

    unroll = GDN_UNROLL
    streams = [(sb, hd) for sb in range(unroll) for hd in heads]

    def block(rr, carry):
        rows = [pl.ds(pl.multiple_of((rr * unroll + sb) * blk, blk), blk) for sb in range(unroll)]
        gate = lambda sb, hd: pl.ds((rr * unroll + sb) * nh + hd, 1)
        qn = [qn_ref[rows[sb], sls[hd]] for sb, hd in streams]
        kn = [kn_ref[rows[sb], sls[hd]] for sb, hd in streams]
        v = [vv_ref[rows[sb], sls[hd]] for sb, hd in streams]
        g_row = [jnp.broadcast_to(gc_ref[gate(sb, hd), :], (blk, blk)) for sb, hd in streams]
        g_col = [x.T for x in g_row]
        b_col = [jnp.broadcast_to(beta_ref[gate(sb, hd), :], (blk, blk)).T for sb, hd in streams]
        decay = [jnp.exp(jnp.where(m_tril, gc_ - gr_, -jnp.inf)) for gc_, gr_ in zip(g_col, g_row)]
        kb = [a * b for a, b in zip(kn, b_col)]
        vb = [a * b for a, b in zip(v, b_col)]
        kn16 = bf(kn)
        kk = [_dot_nt(a, b) for a, b in zip(bf(kb), kn16)]
        qk = [_dot_nt(a, b) for a, b in zip(bf(qn), kn16)]
        a_mat = [jnp.where(m_strict, x * d, 0.0) for x, d in zip(kk, decay)]
        intra16 = bf([x * d for x, d in zip(qk, decay)])
        inv = [eye - a for a in a_mat]
        p16 = bf(a_mat)
        n = 2
        while n < c:
            p16 = bf([_dot(x, x) for x in p16])
            inv = [i + _dot(i16, x) for i, i16, x in zip(inv, bf(inv), p16)]
            n *= 2
        eg = [jnp.exp(x) for x in g_col]
        rhs = bf([jnp.concatenate([a, b * e], axis=1) for a, b, e in zip(vb, kb, eg)])
        uw = [_dot(i16, x) for i16, x in zip(bf(inv), rhs)]
        u = [x[:, :HEAD_DIM] for x in uw]
        w16 = bf([x[:, HEAD_DIM:] for x in uw])
        g_last1 = [jnp.broadcast_to(x[c - 1:c, :], (blk, blk)) for x in g_col]
        g_last2 = [jnp.broadcast_to(x[blk - 1:blk, :], (blk, blk)) for x in g_col]
        kdec16 = bf([k_ * jnp.exp(jnp.where(first, l1, l2) - gc_)
                     for k_, l1, l2, gc_ in zip(kn, g_last1, g_last2, g_col)])
        qg16 = bf([a * e for a, e in zip(qn, eg)])
        e1 = [jnp.exp(x) for x in g_last1]
        e2 = [jnp.exp(x) for x in g_last2]

        state = [state_ref[hd] for hd in heads]
        for sb in range(unroll):
            mine = lambda xs: xs[sb * nh:(sb + 1) * nh]
            u_, w_, q_, k_ = mine(u), mine(w16), mine(qg16), mine(kdec16)
            s016 = bf(state)
            vn1 = [a[0:c] - _dot(b[0:c], s_) for a, b, s_ in zip(u_, w_, s016)]
            oq1 = [_dot(a[0:c], s_) for a, s_ in zip(q_, s016)]
            s1 = [s_ * e_ + _dot_tn(a[0:c], x) for s_, e_, a, x in zip(state, mine(e1), k_, bf(vn1))]
            s116 = bf(s1)
            vn2 = [a[c:blk] - _dot(b[c:blk], s_) for a, b, s_ in zip(u_, w_, s116)]
            oq2 = [_dot(a[c:blk], s_) for a, s_ in zip(q_, s116)]
            state = [s_ * e_ + _dot_tn(a[c:blk], x) for s_, e_, a, x in zip(s1, mine(e2), k_, bf(vn2))]
            for hd in heads:
                vn = jnp.concatenate([vn1[hd], vn2[hd]], axis=0).astype(BF16)
                oacc_ref[rows[sb], sls[hd]] = (jnp.concatenate([oq1[hd], oq2[hd]], axis=0)
                                               + _dot(mine(intra16)[hd], vn))
        for hd in heads:
            state_ref[hd] = state[hd]
        return carry

    lax.fori_loop(0, T // (blk * unroll), block, 0)

    nw = nw_ref[...]
    for hd in range(nh):
        sl = slice(hd * HEAD_DIM, (hd + 1) * HEAD_DIM)
        o = oacc_ref[:, sl]
        zz = z_ref[0, :, sl].astype(F32)
        y = o * lax.rsqrt(jnp.mean(o * o, axis=-1, keepdims=True) + EPS) * nw
        o_ref[0, :, sl] = (y * (zz * _sigmoid(zz))).astype(o_ref.dtype)


def _gated_delta(proj3, gates, conv_w, a_log, dt_bias, norm_w):
    b, s, _ = proj3.shape
    hh = GDN_HEADS
    T = GDN_T
    tiles = T // LANES
    width = hh * HEAD_DIM
    gates4 = gates.reshape(b, s // LANES, 2 * hh, LANES)
    lane_bcast = lambda p: jnp.broadcast_to(p.reshape(hh, 1), (hh, LANES))
    col_spec = lambda j: pl.BlockSpec((1, T, width), lambda bi, ti: (bi, ti, j))
    gate_spec = lambda j: pl.BlockSpec((1, tiles, hh, LANES), lambda bi, ti: (bi, ti, j, 0))
    cw_spec = lambda j: pl.BlockSpec((GDN_CONV, width), lambda bi, ti: (0, j))
    head_spec = pl.BlockSpec((hh, LANES), lambda bi, ti: (0, 0))
    buf = pltpu.VMEM((T + GDN_BLK, width), BF16)
    full = pltpu.VMEM((T, width), F32)
    gate_buf = pltpu.VMEM((tiles * hh, LANES), F32)
    return pl.pallas_call(
        _gdn_kernel,
        grid=(b, s // T),
        in_specs=[head_spec, head_spec,
                  col_spec(0), col_spec(1), col_spec(2), col_spec(3),
                  gate_spec(1), gate_spec(0),
                  cw_spec(0), cw_spec(1), cw_spec(2),
                  pl.BlockSpec((1, HEAD_DIM), lambda bi, ti: (0, 0))],
        out_specs=pl.BlockSpec((1, T, width), lambda bi, ti: (bi, ti, 0)),
        out_shape=jax.ShapeDtypeStruct((b, s, width), BF16),
        scratch_shapes=[pltpu.VMEM((hh, HEAD_DIM, HEAD_DIM), F32), buf, buf, buf, full, full, full,
                        gate_buf, gate_buf, full],
        compiler_params=_cparams(("parallel", "arbitrary")),
        name="gated_delta",
    )(lane_bcast(a_log), lane_bcast(dt_bias), proj3, proj3, proj3, proj3, gates4, gates4,
      conv_w, conv_w, conv_w, norm_w.reshape(1, HEAD_DIM))


def _rms_rows(x, w):
    ms = jnp.mean(x * x, axis=-1, keepdims=True)
    return x * lax.rsqrt(ms + EPS) * w


def _qproj_kernel(c_ref, nw_ref, w_ref, cos_ref, sin_ref, o_ref):
    hq = _rms_rows(c_ref[0].astype(F32), nw_ref[...]).astype(BF16)
    res = _dot(hq, w_ref[...])
    cos = cos_ref[0]
    sin = sin_ref[0]
    scale = SOFTMAX_SCALE * LOG2E
    for hd in range(MLA_HEADS):
        base = hd * QK_PAD
        nope = res[:, base:base + QK_NOPE]
        rp = res[:, base + QK_NOPE:base + QK_PAD]
        rot = rp * cos + _rotate_half_unsigned(rp) * sin
        o_ref[0, hd, :, 0:QK_NOPE] = (nope * scale).astype(BF16)
        o_ref[0, hd, :, QK_NOPE:QK_PAD] = (rot * scale).astype(BF16)


def _kvproj_kernel(c_ref, small_ref, nw_ref, w_ref, cos_ref, sin_ref, k_ref, v_ref):
    hkv = _rms_rows(c_ref[0].astype(F32), nw_ref[...]).astype(BF16)
    res = _dot(hkv, w_ref[...])
    cos = cos_ref[0]
    sin = sin_ref[0]
    kr = small_ref[0]
    kr16 = (kr * cos + _rotate_half_unsigned(kr) * sin).astype(BF16)
    nk = MLA_HEADS * QK_NOPE
    for hd in range(MLA_HEADS):
        k_ref[0, hd, :, 0:QK_NOPE] = res[:, hd * QK_NOPE:(hd + 1) * QK_NOPE].astype(BF16)
        k_ref[0, hd, :, QK_NOPE:QK_PAD] = kr16
        v_ref[0, hd] = res[:, nk + hd * V_HEAD:nk + (hd + 1) * V_HEAD].astype(BF16)


def _mla_projections(proj3, small3, q_norm_w, w_uq, kv_norm_w, w_ukv, cos_tab, sin_tab):
    b, s, _ = proj3.shape
    hh = MLA_HEADS
    tm = MLA_TM
    main_cq = (3 * GDN_HEADS * HEAD_DIM + GDN_HEADS * HEAD_DIM) // Q_LORA
    main_ckv = main_cq + 1
    wq = w_uq.reshape(Q_LORA, hh, QK_NOPE + QK_ROPE)
    gap = jnp.zeros((Q_LORA, hh, QK_PAD - QK_NOPE - QK_ROPE - QK_ROPE // 2), w_uq.dtype)
    wq = jnp.concatenate([wq[:, :, :QK_NOPE], _with_x2_duplicate(wq[:, :, QK_NOPE:], gap)], axis=2)
    wq = wq.reshape(Q_LORA, hh * QK_PAD).astype(BF16)
    wkv = w_ukv.reshape(KV_LORA, hh, QK_NOPE + V_HEAD)
    wkv = jnp.concatenate([wkv[:, :, :QK_NOPE].reshape(KV_LORA, hh * QK_NOPE),
                           wkv[:, :, QK_NOPE:].reshape(KV_LORA, hh * V_HEAD)], axis=1).astype(BF16)
    tab_spec = pl.BlockSpec((1, tm, LANES), lambda bi, ti: (bi, ti, 0))
    q = pl.pallas_call(
        _qproj_kernel,
        grid=(b, s // tm),
        in_specs=[pl.BlockSpec((1, tm, Q_LORA), lambda bi, ti: (bi, ti, main_cq)),
                  pl.BlockSpec((1, Q_LORA), lambda bi, ti: (0, 0)),
                  pl.BlockSpec((Q_LORA, hh * QK_PAD), lambda bi, ti: (0, 0)),
                  tab_spec, tab_spec],
        out_specs=pl.BlockSpec((1, hh, tm, QK_PAD), lambda bi, ti: (bi, 0, ti, 0)),
        out_shape=jax.ShapeDtypeStruct((b, hh, s, QK_PAD), BF16),
        compiler_params=_cparams(("parallel", "parallel")),
        name="mla_q_proj",
    )(proj3, q_norm_w.reshape(1, Q_LORA), wq, cos_tab, sin_tab)
    k, v = pl.pallas_call(
        _kvproj_kernel,
        grid=(b, s // tm),
        in_specs=[pl.BlockSpec((1, tm, KV_LORA), lambda bi, ti: (bi, ti, main_ckv)),
                  tab_spec,
                  pl.BlockSpec((1, KV_LORA), lambda bi, ti: (0, 0)),
                  pl.BlockSpec((KV_LORA, hh * (QK_NOPE + V_HEAD)), lambda bi, ti: (0, 0)),
                  tab_spec, tab_spec],
        out_specs=[pl.BlockSpec((1, hh, tm, QK_PAD), lambda bi, ti: (bi, 0, ti, 0)),
                   pl.BlockSpec((1, hh, tm, V_HEAD), lambda bi, ti: (bi, 0, ti, 0))],
        out_shape=[jax.ShapeDtypeStruct((b, hh, s, QK_PAD), BF16),
                   jax.ShapeDtypeStruct((b, hh, s, V_HEAD), BF16)],
        compiler_params=_cparams(("parallel", "parallel")),
        name="mla_kv_proj",
    )(proj3, small3, kv_norm_w.reshape(1, KV_LORA), wkv, cos_tab, sin_tab)
    return q, k, v


def _attn_kernel(q_ref, k_ref, v_ref, nw_ref, o_ref, *scratch):
    nh = ATT_HEADS
    tk = ATT_TK
    streams = [(hd, half) for hd in range(nh) for half in range(2)]
    ns = len(streams)
    s_refs, (m_ref, acc_ref) = scratch[:ns], scratch[ns:]
    i = pl.program_id(2)
    kb = 2 * tk
    q = [q_ref[0, hd, half * tk:(half + 1) * tk, :] for hd, half in streams]
    m_ref[...] = jnp.full(m_ref.shape, -jnp.inf, F32)
    acc_ref[...] = jnp.zeros(acc_ref.shape, F32)

    def scores(st, start, n):
        return _dot_nt(q[st], k_ref[0, streams[st][0], pl.ds(start, n), :])

    def update(st, sc, start, n, visible_past=None):
        if visible_past is not None:
            qpos = lax.broadcasted_iota(jnp.int32, sc.shape, 0)
            kpos = lax.broadcasted_iota(jnp.int32, sc.shape, 1)
            sc = jnp.where(kpos <= qpos + visible_past, sc, -jnp.inf)
        vb = v_ref[0, streams[st][0], pl.ds(start, n), :]
        vb = jnp.concatenate([vb, jnp.ones((n, V_HEAD), BF16)], axis=1)
        m_prev = m_ref[st]
        m_new = jnp.maximum(m_prev, jnp.max(sc, axis=-1, keepdims=True))
        p = jnp.exp2(sc - jnp.concatenate([m_new] * (n // LANES), axis=1))
        alpha = jnp.exp2(m_prev - m_new)
        acc_ref[st] = jnp.concatenate([alpha, alpha], axis=1) * acc_ref[st] + _dot(p.astype(BF16), vb)
        m_ref[st] = m_new

    every = range(ns)

    def body(jj, carry):
        start = pl.multiple_of(jj * kb, kb)
        for st in every:
            s_refs[st][...] = scores(st, start, kb)
        for st in every:
            update(st, s_refs[st][...], start, kb)
        return carry

    lax.fori_loop(0, i, body, 0)

    start = pl.multiple_of(i * kb, kb)
    for st in every:
        if streams[st][1] == 0:
            s_refs[st][:, 0:tk] = scores(st, start, tk)
        else:
            s_refs[st][...] = scores(st, start, kb)
    for st in every:
        if streams[st][1] == 0:
            update(st, s_refs[st][:, 0:tk], start, tk, visible_past=0)
        else:
            update(st, s_refs[st][...], start, kb, visible_past=tk)

    nw = nw_ref[...]
    for st, (hd, half) in enumerate(streams):
        acc = acc_ref[st]
        o = acc[:, :V_HEAD] / acc[:, V_HEAD:]
        y = o * lax.rsqrt(jnp.mean(o * o, axis=-1, keepdims=True) + EPS) * nw
        o_ref[0, half * tk:(half + 1) * tk, hd * V_HEAD:(hd + 1) * V_HEAD] = y.astype(o_ref.dtype)


def _attention(q, k, v, norm_w):
    b, hh, s, _ = q.shape
    tk = ATT_TK
    tq = 2 * tk
    nh = ATT_HEADS
    ns = 2 * nh
    score_buf = pltpu.VMEM((tk, 2 * tk), F32)
    return pl.pallas_call(
        _attn_kernel,
        grid=(b, hh // nh, s // tq),
        in_specs=[pl.BlockSpec((1, nh, tq, QK_PAD), lambda bi, hi, qi: (bi, hi, qi, 0)),
                  pl.BlockSpec((1, nh, s, QK_PAD), lambda bi, hi, qi: (bi, hi, 0, 0)),
                  pl.BlockSpec((1, nh, s, V_HEAD), lambda bi, hi, qi: (bi, hi, 0, 0)),
                  pl.BlockSpec((1, V_HEAD), lambda bi, hi, qi: (0, 0))],
        out_specs=pl.BlockSpec((1, tq, nh * V_HEAD), lambda bi, hi, qi: (bi, qi, hi)),
        out_shape=jax.ShapeDtypeStruct((b, s, hh * V_HEAD), BF16),
        scratch_shapes=[score_buf] * ns + [pltpu.VMEM((ns, tk, V_HEAD), F32),
                                            pltpu.VMEM((ns, tk, 2 * V_HEAD), F32)],
        compiler_params=_cparams(("parallel", "parallel", "arbitrary")),
        name="mla_attention",
    )(q, k, v, norm_w.reshape(1, V_HEAD))


def _outproj_kernel(og_ref, om_ref, x_ref, wg_ref, wm_ref, o_ref):
    o_ref[...] = x_ref[...] + _dot(og_ref[...], wg_ref[...]) + _dot(om_ref[...], wm_ref[...])


def _out_projection(o_gdn, o_mla, x2, w_out):
    m, d = x2.shape
    kg = o_gdn.shape[1]
    km = o_mla.shape[1]
    tm = OUT_TM
    w16 = w_out.astype(BF16)
    return pl.pallas_call(
        _outproj_kernel,
        grid=(m // tm,),
        in_specs=[pl.BlockSpec((tm, kg), lambda i: (i, 0)),
                  pl.BlockSpec((tm, km), lambda i: (i, 0)),
                  pl.BlockSpec((tm, d), lambda i: (i, 0)),
                  pl.BlockSpec((kg, d), lambda i: (0, 0)),
                  pl.BlockSpec((km, d), lambda i: (0, 0))],
        out_specs=pl.BlockSpec((tm, d), lambda i: (i, 0)),
        out_shape=jax.ShapeDtypeStruct((m, d), F32),
        compiler_params=_cparams(("parallel",)),
        name="out_projection",
    )(o_gdn, o_mla, x2, w16[:kg], w16[kg:])


def _ffn_kernel(x_ref, nw_ref, wg_ref, wu_ref, wd_ref, fw_ref, o_ref, h_ref, *, nj):
    j = pl.program_id(1)

    tm = x_ref.shape[0]

    @pl.when(j == 0)
    def _():
        x = x_ref[...]
        h_ref[...] = _rms_rows(x, nw_ref[...]).astype(BF16)
        o_ref[...] = x

    halves = [slice(0, tm // 2), slice(tm // 2, tm)]
    wg = wg_ref[...]
    wu = wu_ref[...]
    gu = [(_dot(h_ref[rows, :], wg), _dot(h_ref[rows, :], wu)) for rows in halves]
    act = [(g * _sigmoid(g) * u).astype(BF16) for g, u in gu]
    wd = wd_ref[...]
    for rows, a in zip(halves, act):
        o_ref[rows, :] += _dot(a, wd)

    @pl.when(j == nj - 1)
    def _():
        o_ref[...] = _rms_rows(o_ref[...], fw_ref[...])


def _ffn(x1, norm_w, w_gate, w_up, w_down, final_w):
    m, d = x1.shape
    dff = w_gate.shape[1]
    tm = FFN_TM
    tf = FFN_TF
    nj = dff // tf
    kern = functools.partial(_ffn_kernel, nj=nj)
    return pl.pallas_call(
        kern,
        grid=(m // tm, nj),
        in_specs=[pl.BlockSpec((tm, d), lambda i, j: (i, 0)),
                  pl.BlockSpec((1, d), lambda i, j: (0, 0)),
                  pl.BlockSpec((d, tf), lambda i, j: (0, j)),
                  pl.BlockSpec((d, tf), lambda i, j: (0, j)),
                  pl.BlockSpec((tf, d), lambda i, j: (j, 0)),
                  pl.BlockSpec((1, d), lambda i, j: (0, 0))],
        out_specs=pl.BlockSpec((tm, d), lambda i, j: (i, 0)),
        out_shape=jax.ShapeDtypeStruct((m, d), F32),
        scratch_shapes=[pltpu.VMEM((tm, d), BF16)],
        compiler_params=_cparams(("parallel", "arbitrary")),
        name="swiglu_ffn",
    )(x1, norm_w.reshape(1, d), w_gate.astype(BF16), w_up.astype(BF16), w_down.astype(BF16),
      final_w.reshape(1, d))


def kernel(x, positions, attn_norm_w, w_in, conv_w, a_log, dt_bias, gdn_norm_w, q_norm_w, w_uq,
           kv_norm_w, w_ukv, mla_out_norm_w, w_out, ffn_norm_w, w_gate, w_up, w_down, final_norm_w):
    b, s, d = x.shape
    assert w_in.shape[0] == 1, "the final rmsnorm is fused into the (single) layer's FFN kernel"
    l = 0
    cos_tab, sin_tab = _rope_tables(positions)
    x2 = x.reshape(b * s, d)
    proj, small, gates = _in_projection(x2, attn_norm_w[l], w_in[l])
    proj3 = proj.reshape(b, s, proj.shape[1])
    small3 = small.reshape(b, s, LANES)
    o_gdn = _gated_delta(proj3, gates, conv_w[l], a_log[l], dt_bias[l], gdn_norm_w[l])
    q, k, v = _mla_projections(proj3, small3, q_norm_w[l], w_uq[l], kv_norm_w[l], w_ukv[l],
                               cos_tab, sin_tab)
    o_mla = _attention(q, k, v, mla_out_norm_w[l])
    x1 = _out_projection(o_gdn.reshape(b * s, -1), o_mla.reshape(b * s, -1), x2, w_out[l])
    out = _ffn(x1, ffn_norm_w[l], w_gate[l], w_up[l], w_down[l], final_norm_w)
    return out.reshape(b, s, d)
```

```python
import functools
import math

import jax
import jax.numpy as jnp
from jax import lax
from jax.experimental import pallas as pl
from jax.experimental.pallas import tpu as pltpu

F32 = jnp.float32
BF16 = jnp.bfloat16

GDN_HEADS = 8
HEAD_DIM = 128
GDN_CONV = 4
GDN_CHUNK = 64
MLA_HEADS = 8
QK_NOPE = 128
QK_ROPE = 64
V_HEAD = 128
Q_LORA = 512
KV_LORA = 512
ROPE_THETA = 10000.0
EPS = 1e-6

LANES = 128
SUBLANES = 8
VMEM_LIMIT_BYTES = 56 * 2**20

INPROJ_TM = 1024
INPROJ_TN = 1792
GDN_T = 512
GDN_BLK = 2 * GDN_CHUNK
GDN_UNROLL = 2
MLA_TM = 1024
ATT_TK = 512
ATT_HEADS = 2
OUT_TM = 512
FFN_TM = 1024
FFN_TF = 512

QK_PAD = 256
SOFTMAX_SCALE = (QK_NOPE + QK_ROPE) ** -0.5
LOG2E = math.log2(math.e)


def _cparams(semantics):
    return pltpu.CompilerParams(dimension_semantics=semantics, vmem_limit_bytes=VMEM_LIMIT_BYTES)


def _sigmoid(x):
    return 1.0 / (1.0 + jnp.exp(-x))


def _softplus(x):
    return jnp.maximum(x, 0.0) + jnp.log1p(jnp.exp(-jnp.abs(x)))


def _dot(a, b):
    return jnp.dot(a, b, preferred_element_type=F32)


def _dot_nt(a, b):
    return lax.dot_general(a, b, (((1,), (1,)), ((), ())), preferred_element_type=F32)


def _dot_tn(a, b):
    return lax.dot_general(a, b, (((0,), (0,)), ((), ())), preferred_element_type=F32)


def _rope_kernel(pos_ref, invf_ref, cos_ref, sin_ref):
    half = QK_ROPE // 2
    per_row = LANES // half
    rows = pos_ref.shape[0]
    ang = pos_ref[...].astype(F32) * invf_ref[...]
    cos = jnp.cos(ang)
    sin = jnp.sin(ang)
    lane = lax.broadcasted_iota(jnp.int32, ang.shape, 1)
    for g in range(per_row):
        to_front = lambda x: pltpu.roll(x, LANES - g * half, axis=1) if g else x
        c1 = to_front(cos)
        s1 = to_front(sin)
        c2 = pltpu.roll(c1, half, axis=1)
        s2 = pltpu.roll(s1, half, axis=1)
        token_rows = pl.ds(g, rows, stride=per_row)
        cos_ref[token_rows, :] = jnp.where(lane < half, c1, jnp.where(lane < QK_ROPE, c2, 0.0))
        sin_ref[token_rows, :] = jnp.where(lane < half, -s1, jnp.where(lane < QK_ROPE, s2, 0.0))


def _rope_tables(positions):
    b, s = positions.shape
    m = b * s
    half = QK_ROPE // 2
    per_row = LANES // half
    rows = m // per_row
    blk = min(rows, 1024)
    inv_freq = ROPE_THETA ** (-jnp.arange(half, dtype=F32) / half)
    invf = jnp.tile(inv_freq, per_row).reshape(1, LANES)
    pos_rep = jnp.broadcast_to(positions.reshape(rows, per_row, 1), (rows, per_row, half)).reshape(rows, LANES)
    cos_tab, sin_tab = pl.pallas_call(
        _rope_kernel,
        grid=(rows // blk,),
        in_specs=[pl.BlockSpec((blk, LANES), lambda i: (i, 0)),
                  pl.BlockSpec((1, LANES), lambda i: (0, 0))],
        out_specs=[pl.BlockSpec((blk * per_row, LANES), lambda i: (i, 0)),
                   pl.BlockSpec((blk * per_row, LANES), lambda i: (i, 0))],
        out_shape=[jax.ShapeDtypeStruct((m, LANES), F32)] * 2,
        compiler_params=_cparams(("parallel",)),
        name="rope_table",
    )(pos_rep, invf)
    return cos_tab.reshape(b, s, LANES), sin_tab.reshape(b, s, LANES)


def _rotate_half_unsigned(x):
    return pltpu.roll(x, QK_ROPE // 2, axis=1)


def _with_x2_duplicate(w_rope, gap):
    return jnp.concatenate([w_rope, gap, w_rope[..., QK_ROPE // 2:]], axis=-1)


def _inproj_kernel(x_ref, nw_ref, w_ref, o_ref, small_ref, gates_ref, h_ref, *, nj, small_off):
    j = pl.program_id(1)

    @pl.when(j == 0)
    def _():
        x = x_ref[...]
        ms = jnp.mean(x * x, axis=-1, keepdims=True)
        h_ref[...] = (x * lax.rsqrt(ms + EPS) * nw_ref[...]).astype(BF16)

    acc = _dot(h_ref[...], w_ref[...])
    o_ref[...] = acc.astype(BF16)

    @pl.when(j == nj - 1)
    def _():
        small = acc[:, small_off:small_off + LANES]
        small_ref[...] = small
        for r in range(gates_ref.shape[0]):
            tile = small[r * LANES:(r + 1) * LANES, :].T
            gates_ref[r] = tile[QK_ROPE:QK_ROPE + 2 * GDN_HEADS, :]


def _in_projection(x2, norm_w, w_in):
    m, d = x2.shape
    conv_ch = 3 * GDN_HEADS * HEAD_DIM
    gdn_v = GDN_HEADS * HEAD_DIM
    w_in = w_in.astype(BF16)
    o = 0
    qkv = w_in[:, o:o + conv_ch]; o += conv_ch
    z = w_in[:, o:o + gdn_v]; o += gdn_v
    b_raw = w_in[:, o:o + GDN_HEADS]; o += GDN_HEADS
    a_raw = w_in[:, o:o + GDN_HEADS]; o += GDN_HEADS
    cq = w_in[:, o:o + Q_LORA]; o += Q_LORA
    ckv = w_in[:, o:o + KV_LORA]; o += KV_LORA
    kr = w_in[:, o:o + QK_ROPE]
    main_w = conv_ch + gdn_v + Q_LORA + KV_LORA
    tn = INPROJ_TN
    total = pl.cdiv(main_w + LANES, tn) * tn
    gates_w = jnp.concatenate([b_raw, a_raw], axis=1)
    gap = jnp.zeros((d, LANES - QK_ROPE - 2 * GDN_HEADS - QK_ROPE // 2), w_in.dtype)
    small_w = _with_x2_duplicate(kr, jnp.concatenate([gates_w, gap], axis=1))
    pad = jnp.zeros((d, total - main_w - LANES), w_in.dtype)
    w = jnp.concatenate([qkv, z, cq, ckv, small_w, pad], axis=1)
    nj = total // tn
    small_off = main_w - (nj - 1) * tn
    tm = INPROJ_TM
    kern = functools.partial(_inproj_kernel, nj=nj, small_off=small_off)
    return pl.pallas_call(
        kern,
        grid=(m // tm, nj),
        in_specs=[pl.BlockSpec((tm, d), lambda i, j: (i, 0)),
                  pl.BlockSpec((1, d), lambda i, j: (0, 0)),
                  pl.BlockSpec((d, tn), lambda i, j: (0, j))],
        out_specs=[pl.BlockSpec((tm, tn), lambda i, j: (i, j)),
                   pl.BlockSpec((tm, LANES), lambda i, j: (i, 0)),
                   pl.BlockSpec((tm // LANES, 2 * GDN_HEADS, LANES), lambda i, j: (i, 0, 0))],
        out_shape=[jax.ShapeDtypeStruct((m, total), BF16),
                   jax.ShapeDtypeStruct((m, LANES), F32),
                   jax.ShapeDtypeStruct((m // LANES, 2 * GDN_HEADS, LANES), F32)],
        scratch_shapes=[pltpu.VMEM((tm, d), BF16)],
        compiler_params=_cparams(("parallel", "arbitrary")),
        name="in_projection",
    )(x2, norm_w.reshape(1, d), w)


def _gdn_kernel(alog_ref, dtb_ref, q_ref, k_ref, v_ref, z_ref, a_ref, b_ref,
                cwq_ref, cwk_ref, cwv_ref, nw_ref, o_ref,
                state_ref, xq_ref, xk_ref, xv_ref, qn_ref, kn_ref, vv_ref, gc_ref, beta_ref):
    t = pl.program_id(1)
    T = q_ref.shape[1]
    nh = GDN_HEADS
    blk = GDN_BLK
    hist = blk
    c = GDN_CHUNK

    @pl.when(t == 0)
    def _():
        state_ref[...] = jnp.zeros_like(state_ref)
        zero_hist = jnp.zeros((hist, nh * HEAD_DIM), BF16)
        xq_ref[0:hist, :] = zero_hist
        xk_ref[0:hist, :] = zero_hist
        xv_ref[0:hist, :] = zero_hist

    taps = GDN_CONV - 1
    sel_row = lax.broadcasted_iota(jnp.int32, (taps * blk, hist + blk), 0)
    sel_col = lax.broadcasted_iota(jnp.int32, (taps * blk, hist + blk), 1)
    shift_sel = jnp.where(sel_col == sel_row % blk + hist - taps + sel_row // blk, 1.0, 0.0).astype(BF16)

    def conv_silu(xbuf, cw_ref, m):
        window = xbuf[m * blk:m * blk + hist + blk, :]
        w = cw_ref[...]
        shifted = _dot(shift_sel, window)
        y = window[hist:, :].astype(F32) * w[taps:taps + 1, :]
        for i in range(taps):
            y = y + shifted[i * blk:(i + 1) * blk, :] * w[i:i + 1, :]
        return y * _sigmoid(y)

    for in_ref, xbuf in ((q_ref, xq_ref), (k_ref, xk_ref), (v_ref, xv_ref)):
        xbuf[hist:hist + T, :] = in_ref[0]
    for m in range(T // blk):
        rows = slice(m * blk, (m + 1) * blk)
        q = conv_silu(xq_ref, cwq_ref, m)
        k = conv_silu(xk_ref, cwk_ref, m)
        vv_ref[rows, :] = conv_silu(xv_ref, cwv_ref, m)
        for hd in range(nh):
            sl = slice(hd * HEAD_DIM, (hd + 1) * HEAD_DIM)
            qh = q[:, sl]
            kh = k[:, sl]
            qn_ref[rows, sl] = qh * lax.rsqrt(jnp.sum(qh * qh, axis=-1, keepdims=True) + EPS) * (HEAD_DIM ** -0.5)
            kn_ref[rows, sl] = kh * lax.rsqrt(jnp.sum(kh * kh, axis=-1, keepdims=True) + EPS)
    for xbuf in (xq_ref, xk_ref, xv_ref):
        xbuf[0:hist, :] = xbuf[T:T + hist, :]

    neg_a = -jnp.exp(alog_ref[...])
    dtb = dtb_ref[...]
    lane = lax.broadcasted_iota(jnp.int32, (nh, LANES), 1)
    in_chunk = lane % c
    for r in range(T // LANES):
        tile = pl.ds(r * nh, nh)
        beta_ref[tile, :] = _sigmoid(b_ref[0, r])
        gc = neg_a * _softplus(a_ref[0, r] + dtb)
        s = 1
        while s < c:
            gc = gc + jnp.where(in_chunk >= s, pltpu.roll(gc, s, axis=1), 0.0)
            s *= 2
        gc_ref[tile, :] = gc

    row = lax.broadcasted_iota(jnp.int32, (blk, blk), 0)
    col = lax.broadcasted_iota(jnp.int32, (blk, blk), 1)
    same_chunk = (row // c) == (col // c)
    m_tril = jnp.logical_and(same_chunk, row >= col)
    m_strict = jnp.logical_and(same_chunk, row > col)
    eye = jnp.where(row == col, 1.0, 0.0).astype(F32)
    first = row < c

    heads = range(nh)
    sls = [slice(hd * HEAD_DIM, (hd + 1) * HEAD_DIM) for hd in heads]
    bf = lambda xs: [x.astype(BF16) for x in xs]

    unroll = GDN_UNROLL
    streams = [(sb, hd) for sb in range(unroll) for hd in heads]

    def block(rr, carry):
        rows = [pl.ds(pl.multiple_of((rr * unroll + sb) * blk, blk), blk) for sb in range(unroll)]
        gate = lambda sb, hd: pl.ds((rr * unroll + sb) * nh + hd, 1)
        qn = [qn_ref[rows[sb], sls[hd]] for sb, hd in streams]
        kn = [kn_ref[rows[sb], sls[hd]] for sb, hd in streams]
        v = [vv_ref[rows[sb], sls[hd]] for sb, hd in streams]
        g_row = [jnp.broadcast_to(gc_ref[gate(sb, hd), :], (blk, blk)) for sb, hd in streams]
        g_col = [x.T for x in g_row]
        b_col = [jnp.broadcast_to(beta_ref[gate(sb, hd), :], (blk, blk)).T for sb, hd in streams]
        decay = [jnp.exp(jnp.where(m_tril, gc_ - gr_, -jnp.inf)) for gc_, gr_ in zip(g_col, g_row)]
        kb = [a * b for a, b in zip(kn, b_col)]
        vb = [a * b for a, b in zip(v, b_col)]
        kn16 = bf(kn)
        kk = [_dot_nt(a, b) for a, b in zip(bf(kb), kn16)]
        qk = [_dot_nt(a, b) for a, b in zip(bf(qn), kn16)]
        a_mat = [jnp.where(m_strict, x * d, 0.0) for x, d in zip(kk, decay)]
        intra16 = bf([x * d for x, d in zip(qk, decay)])
        inv = [eye - a for a in a_mat]
        p16 = bf(a_mat)
        n = 2
        while n < c:
            p16 = bf([_dot(x, x) for x in p16])
            inv = [i + _dot(i16, x) for i, i16, x in zip(inv, bf(inv), p16)]
            n *= 2
        eg = [jnp.exp(x) for x in g_col]
        rhs = bf([jnp.concatenate([a, b * e], axis=1) for a, b, e in zip(vb, kb, eg)])
        uw = [_dot(i16, x) for i16, x in zip(bf(inv), rhs)]
        u = [x[:, :HEAD_DIM] for x in uw]
        w16 = bf([x[:, HEAD_DIM:] for x in uw])
        g_last1 = [jnp.broadcast_to(x[c - 1:c, :], (blk, blk)) for x in g_col]
        g_last2 = [jnp.broadcast_to(x[blk - 1:blk, :], (blk, blk)) for x in g_col]
        kdec16 = bf([k_ * jnp.exp(jnp.where(first, l1, l2) - gc_)
                     for k_, l1, l2, gc_ in zip(kn, g_last1, g_last2, g_col)])
        qg16 = bf([a * e for a, e in zip(qn, eg)])
        e1 = [jnp.exp(x) for x in g_last1]
        e2 = [jnp.exp(x) for x in g_last2]

        state = [state_ref[hd] for hd in heads]
        for sb in range(unroll):
            mine = lambda xs: xs[sb * nh:(sb + 1) * nh]
            u_, w_, q_, k_ = mine(u), mine(w16), mine(qg16), mine(kdec16)
            s016 = bf(state)
            vn1 = [a[0:c] - _dot(b[0:c], s_) for a, b, s_ in zip(u_, w_, s016)]
            oq1 = [_dot(a[0:c], s_) for a, s_ in zip(q_, s016)]
            s1 = [s_ * e_ + _dot_tn(a[0:c], x) for s_, e_, a, x in zip(state, mine(e1), k_, bf(vn1))]
            s116 = bf(s1)
            vn2 = [a[c:blk] - _dot(b[c:blk], s_) for a, b, s_ in zip(u_, w_, s116)]
            oq2 = [_dot(a[c:blk], s_) for a, s_ in zip(q_, s116)]
            state = [s_ * e_ + _dot_tn(a[c:blk], x) for s_, e_, a, x in zip(s1, mine(e2), k_, bf(vn2))]
            for hd in heads:
                vn = jnp.concatenate([vn1[hd], vn2[hd]], axis=0).astype(BF16)
                o = jnp.concatenate([oq1[hd], oq2[hd]], axis=0) + _dot(mine(intra16)[hd], vn)
                zz = z_ref[0, rows[sb], sls[hd]].astype(F32)
                y = o * lax.rsqrt(jnp.mean(o * o, axis=-1, keepdims=True) + EPS) * nw
                o_ref[0, rows[sb], sls[hd]] = (y * (zz * _sigmoid(zz))).astype(o_ref.dtype)
        for hd in heads:
            state_ref[hd] = state[hd]
        return carry

    nw = nw_ref[...]
    lax.fori_loop(0, T // (blk * unroll), block, 0)


def _gated_delta(proj3, gates, conv_w, a_log, dt_bias, norm_w):
    b, s, _ = proj3.shape
    hh = GDN_HEADS
    T = GDN_T
    tiles = T // LANES
    width = hh * HEAD_DIM
    gates4 = gates.reshape(b, s // LANES, 2 * hh, LANES)
    lane_bcast = lambda p: jnp.broadcast_to(p.reshape(hh, 1), (hh, LANES))
    col_spec = lambda j: pl.BlockSpec((1, T, width), lambda bi, ti: (bi, ti, j))
    gate_spec = lambda j: pl.BlockSpec((1, tiles, hh, LANES), lambda bi, ti: (bi, ti, j, 0))
    cw_spec = lambda j: pl.BlockSpec((GDN_CONV, width), lambda bi, ti: (0, j))
    head_spec = pl.BlockSpec((hh, LANES), lambda bi, ti: (0, 0))
    buf = pltpu.VMEM((T + GDN_BLK, width), BF16)
    full = pltpu.VMEM((T, width), F32)
    gate_buf = pltpu.VMEM((tiles * hh, LANES), F32)
    return pl.pallas_call(
        _gdn_kernel,
        grid=(b, s // T),
        in_specs=[head_spec, head_spec,
                  col_spec(0), col_spec(1), col_spec(2), col_spec(3),
                  gate_spec(1), gate_spec(0),
                  cw_spec(0), cw_spec(1), cw_spec(2),
                  pl.BlockSpec((1, HEAD_DIM), lambda bi, ti: (0, 0))],
        out_specs=pl.BlockSpec((1, T, width), lambda bi, ti: (bi, ti, 0)),
        out_shape=jax.ShapeDtypeStruct((b, s, width), BF16),
        scratch_shapes=[pltpu.VMEM((hh, HEAD_DIM, HEAD_DIM), F32), buf, buf, buf, full, full, full,
                        gate_buf, gate_buf],
        compiler_params=_cparams(("parallel", "arbitrary")),
        name="gated_delta",
    )(lane_bcast(a_log), lane_bcast(dt_bias), proj3, proj3, proj3, proj3, gates4, gates4,
      conv_w, conv_w, conv_w, norm_w.reshape(1, HEAD_DIM))


def _rms_rows(x, w):
    ms = jnp.mean(x * x, axis=-1, keepdims=True)
    return x * lax.rsqrt(ms + EPS) * w


def _qproj_kernel(c_ref, nw_ref, w_ref, cos_ref, sin_ref, o_ref):
    hq = _rms_rows(c_ref[0].astype(F32), nw_ref[...]).astype(BF16)
    res = _dot(hq, w_ref[...])
    cos = cos_ref[0]
    sin = sin_ref[0]
    scale = SOFTMAX_SCALE * LOG2E
    for hd in range(MLA_HEADS):
        base = hd * QK_PAD
        nope = res[:, base:base + QK_NOPE]
        rp = res[:, base + QK_NOPE:base + QK_PAD]
        rot = rp * cos + _rotate_half_unsigned(rp) * sin
        o_ref[0, hd, :, 0:QK_NOPE] = (nope * scale).astype(BF16)
        o_ref[0, hd, :, QK_NOPE:QK_PAD] = (rot * scale).astype(BF16)


def _kvproj_kernel(c_ref, small_ref, nw_ref, w_ref, cos_ref, sin_ref, k_ref, v_ref):
    hkv = _rms_rows(c_ref[0].astype(F32), nw_ref[...]).astype(BF16)
    res = _dot(hkv, w_ref[...])
    cos = cos_ref[0]
    sin = sin_ref[0]
    kr = small_ref[0]
    kr16 = (kr * cos + _rotate_half_unsigned(kr) * sin).astype(BF16)
    nk = MLA_HEADS * QK_NOPE
    for hd in range(MLA_HEADS):
        k_ref[0, hd, :, 0:QK_NOPE] = res[:, hd * QK_NOPE:(hd + 1) * QK_NOPE].astype(BF16)
        k_ref[0, hd, :, QK_NOPE:QK_PAD] = kr16
        v_ref[0, hd] = res[:, nk + hd * V_HEAD:nk + (hd + 1) * V_HEAD].astype(BF16)


def _mla_projections(proj3, small3, q_norm_w, w_uq, kv_norm_w, w_ukv, cos_tab, sin_tab):
    b, s, _ = proj3.shape
    hh = MLA_HEADS
    tm = MLA_TM
    main_cq = (3 * GDN_HEADS * HEAD_DIM + GDN_HEADS * HEAD_DIM) // Q_LORA
    main_ckv = main_cq + 1
    wq = w_uq.reshape(Q_LORA, hh, QK_NOPE + QK_ROPE)
    gap = jnp.zeros((Q_LORA, hh, QK_PAD - QK_NOPE - QK_ROPE - QK_ROPE // 2), w_uq.dtype)
    wq = jnp.concatenate([wq[:, :, :QK_NOPE], _with_x2_duplicate(wq[:, :, QK_NOPE:], gap)], axis=2)
    wq = wq.reshape(Q_LORA, hh * QK_PAD).astype(BF16)
    wkv = w_ukv.reshape(KV_LORA, hh, QK_NOPE + V_HEAD)
    wkv = jnp.concatenate([wkv[:, :, :QK_NOPE].reshape(KV_LORA, hh * QK_NOPE),
                           wkv[:, :, QK_NOPE:].reshape(KV_LORA, hh * V_HEAD)], axis=1).astype(BF16)
    tab_spec = pl.BlockSpec((1, tm, LANES), lambda bi, ti: (bi, ti, 0))
    q = pl.pallas_call(
        _qproj_kernel,
        grid=(b, s // tm),
        in_specs=[pl.BlockSpec((1, tm, Q_LORA), lambda bi, ti: (bi, ti, main_cq)),
                  pl.BlockSpec((1, Q_LORA), lambda bi, ti: (0, 0)),
                  pl.BlockSpec((Q_LORA, hh * QK_PAD), lambda bi, ti: (0, 0)),
                  tab_spec, tab_spec],
        out_specs=pl.BlockSpec((1, hh, tm, QK_PAD), lambda bi, ti: (bi, 0, ti, 0)),
        out_shape=jax.ShapeDtypeStruct((b, hh, s, QK_PAD), BF16),
        compiler_params=_cparams(("parallel", "parallel")),
        name="mla_q_proj",
    )(proj3, q_norm_w.reshape(1, Q_LORA), wq, cos_tab, sin_tab)
    k, v = pl.pallas_call(
        _kvproj_kernel,
        grid=(b, s // tm),
        in_specs=[pl.BlockSpec((1, tm, KV_LORA), lambda bi, ti: (bi, ti, main_ckv)),
                  tab_spec,
                  pl.BlockSpec((1, KV_LORA), lambda bi, ti: (0, 0)),
                  pl.BlockSpec((KV_LORA, hh * (QK_NOPE + V_HEAD)), lambda bi, ti: (0, 0)),
                  tab_spec, tab_spec],
        out_specs=[pl.BlockSpec((1, hh, tm, QK_PAD), lambda bi, ti: (bi, 0, ti, 0)),
                   pl.BlockSpec((1, hh, tm, V_HEAD), lambda bi, ti: (bi, 0, ti, 0))],
        out_shape=[jax.ShapeDtypeStruct((b, hh, s, QK_PAD), BF16),
                   jax.ShapeDtypeStruct((b, hh, s, V_HEAD), BF16)],
        compiler_params=_cparams(("parallel", "parallel")),
        name="mla_kv_proj",
    )(proj3, small3, kv_norm_w.reshape(1, KV_LORA), wkv, cos_tab, sin_tab)
    return q, k, v


def _attn_kernel(q_ref, k_ref, v_ref, nw_ref, o_ref, *scratch):
    nh = ATT_HEADS
    tk = ATT_TK
    streams = [(hd, half) for hd in range(nh) for half in range(2)]
    ns = len(streams)
    s_refs, (m_ref, acc_ref) = scratch[:ns], scratch[ns:]
    i = pl.program_id(2)
    kb = 2 * tk
    q = [q_ref[0, hd, half * tk:(half + 1) * tk, :] for hd, half in streams]
    m_ref[...] = jnp.full(m_ref.shape, -jnp.inf, F32)
    acc_ref[...] = jnp.zeros(acc_ref.shape, F32)

    def scores(st, start, n):
        return _dot_nt(q[st], k_ref[0, streams[st][0], pl.ds(start, n), :])

    def update(st, sc, start, n, visible_past=None):
        if visible_past is not None:
            qpos = lax.broadcasted_iota(jnp.int32, sc.shape, 0)
            kpos = lax.broadcasted_iota(jnp.int32, sc.shape, 1)
            sc = jnp.where(kpos <= qpos + visible_past, sc, -jnp.inf)
        vb = v_ref[0, streams[st][0], pl.ds(start, n), :]
        vb = jnp.concatenate([vb, jnp.ones((n, V_HEAD), BF16)], axis=1)
        m_prev = m_ref[st]
        m_new = jnp.maximum(m_prev, jnp.max(sc, axis=-1, keepdims=True))
        p = jnp.exp2(sc - jnp.concatenate([m_new] * (n // LANES), axis=1))
        alpha = jnp.exp2(m_prev - m_new)
        acc_ref[st] = jnp.concatenate([alpha, alpha], axis=1) * acc_ref[st] + _dot(p.astype(BF16), vb)
        m_ref[st] = m_new

    every = range(ns)

    def body(jj, carry):
        start = pl.multiple_of(jj * kb, kb)
        for st in every:
            s_refs[st][...] = scores(st, start, kb)
        for st in every:
            update(st, s_refs[st][...], start, kb)
        return carry

    lax.fori_loop(0, i, body, 0)

    start = pl.multiple_of(i * kb, kb)
    for st in every:
        if streams[st][1] == 0:
            s_refs[st][:, 0:tk] = scores(st, start, tk)
        else:
            s_refs[st][...] = scores(st, start, kb)
    for st in every:
        if streams[st][1] == 0:
            update(st, s_refs[st][:, 0:tk], start, tk, visible_past=0)
        else:
            update(st, s_refs[st][...], start, kb, visible_past=tk)

    nw = nw_ref[...]
    for st, (hd, half) in enumerate(streams):
        acc = acc_ref[st]
        o = acc[:, :V_HEAD] / acc[:, V_HEAD:]
        y = o * lax.rsqrt(jnp.mean(o * o, axis=-1, keepdims=True) + EPS) * nw
        o_ref[0, half * tk:(half + 1) * tk, hd * V_HEAD:(hd + 1) * V_HEAD] = y.astype(o_ref.dtype)


def _attention(q, k, v, norm_w):
    b, hh, s, _ = q.shape
    tk = ATT_TK
    tq = 2 * tk
    nh = ATT_HEADS
    ns = 2 * nh
    score_buf = pltpu.VMEM((tk, 2 * tk), F32)
    return pl.pallas_call(
        _attn_kernel,
        grid=(b, hh // nh, s // tq),
        in_specs=[pl.BlockSpec((1, nh, tq, QK_PAD), lambda bi, hi, qi: (bi, hi, qi, 0)),
                  pl.BlockSpec((1, nh, s, QK_PAD), lambda bi, hi, qi: (bi, hi, 0, 0)),
                  pl.BlockSpec((1, nh, s, V_HEAD), lambda bi, hi, qi: (bi, hi, 0, 0)),
                  pl.BlockSpec((1, V_HEAD), lambda bi, hi, qi: (0, 0))],
        out_specs=pl.BlockSpec((1, tq, nh * V_HEAD), lambda bi, hi, qi: (bi, qi, hi)),
        out_shape=jax.ShapeDtypeStruct((b, s, hh * V_HEAD), BF16),
        scratch_shapes=[score_buf] * ns + [pltpu.VMEM((ns, tk, V_HEAD), F32),
                                            pltpu.VMEM((ns, tk, 2 * V_HEAD), F32)],
        compiler_params=_cparams(("parallel", "parallel", "arbitrary")),
        name="mla_attention",
    )(q, k, v, norm_w.reshape(1, V_HEAD))


def _outproj_kernel(og_ref, om_ref, x_ref, wg_ref, wm_ref, o_ref):
    o_ref[...] = x_ref[...] + _dot(og_ref[...], wg_ref[...]) + _dot(om_ref[...], wm_ref[...])


def _out_projection(o_gdn, o_mla, x2, w_out):
    m, d = x2.shape
    kg = o_gdn.shape[1]
    km = o_mla.shape[1]
    tm = OUT_TM
    w16 = w_out.astype(BF16)
    return pl.pallas_call(
        _outproj_kernel,
        grid=(m // tm,),
        in_specs=[pl.BlockSpec((tm, kg), lambda i: (i, 0)),
                  pl.BlockSpec((tm, km), lambda i: (i, 0)),
                  pl.BlockSpec((tm, d), lambda i: (i, 0)),
                  pl.BlockSpec((kg, d), lambda i: (0, 0)),
                  pl.BlockSpec((km, d), lambda i: (0, 0))],
        out_specs=pl.BlockSpec((tm, d), lambda i: (i, 0)),
        out_shape=jax.ShapeDtypeStruct((m, d), F32),
        compiler_params=_cparams(("parallel",)),
        name="out_projection",
    )(o_gdn, o_mla, x2, w16[:kg], w16[kg:])


def _ffn_kernel(x_ref, nw_ref, wg_ref, wu_ref, wd_ref, fw_ref, o_ref, h_ref, *, nj):
    j = pl.program_id(1)

    tm = x_ref.shape[0]

    @pl.when(j == 0)
    def _():
        x = x_ref[...]
        h_ref[...] = _rms_rows(x, nw_ref[...]).astype(BF16)
        o_ref[...] = x

    halves = [slice(0, tm // 2), slice(tm // 2, tm)]
    wg = wg_ref[...]
    wu = wu_ref[...]
    gu = [(_dot(h_ref[rows, :], wg), _dot(h_ref[rows, :], wu)) for rows in halves]
    act = [(g * _sigmoid(g) * u).astype(BF16) for g, u in gu]
    wd = wd_ref[...]
    for rows, a in zip(halves, act):
        o_ref[rows, :] += _dot(a, wd)

    @pl.when(j == nj - 1)
    def _():
        o_ref[...] = _rms_rows(o_ref[...], fw_ref[...])


def _ffn(x1, norm_w, w_gate, w_up, w_down, final_w):
    m, d = x1.shape
    dff = w_gate.shape[1]
    tm = FFN_TM
    tf = FFN_TF
    nj = dff // tf
    kern = functools.partial(_ffn_kernel, nj=nj)
    return pl.pallas_call(
        kern,
        grid=(m // tm, nj),
        in_specs=[pl.BlockSpec((tm, d), lambda i, j: (i, 0)),
                  pl.BlockSpec((1, d), lambda i, j: (0, 0)),
                  pl.BlockSpec((d, tf), lambda i, j: (0, j)),
                  pl.BlockSpec((d, tf), lambda i, j: (0, j)),
                  pl.BlockSpec((tf, d), lambda i, j: (j, 0)),
                  pl.BlockSpec((1, d), lambda i, j: (0, 0))],
        out_specs=pl.BlockSpec((tm, d), lambda i, j: (i, 0)),
        out_shape=jax.ShapeDtypeStruct((m, d), F32),
        scratch_shapes=[pltpu.VMEM((tm, d), BF16)],
        compiler_params=_cparams(("parallel", "arbitrary")),
        name="swiglu_ffn",
    )(x1, norm_w.reshape(1, d), w_gate.astype(BF16), w_up.astype(BF16), w_down.astype(BF16),
      final_w.reshape(1, d))


def kernel(x, positions, attn_norm_w, w_in, conv_w, a_log, dt_bias, gdn_norm_w, q_norm_w, w_uq,
           kv_norm_w, w_ukv, mla_out_norm_w, w_out, ffn_norm_w, w_gate, w_up, w_down, final_norm_w):
    b, s, d = x.shape
    assert w_in.shape[0] == 1, "the final rmsnorm is fused into the (single) layer's FFN kernel"
    l = 0
    cos_tab, sin_tab = _rope_tables(positions)
    x2 = x.reshape(b * s, d)
    proj, small, gates = _in_projection(x2, attn_norm_w[l], w_in[l])
    proj3 = proj.reshape(b, s, proj.shape[1])
    small3 = small.reshape(b, s, LANES)
    o_gdn = _gated_delta(proj3, gates, conv_w[l], a_log[l], dt_bias[l], gdn_norm_w[l])
    q, k, v = _mla_projections(proj3, small3, q_norm_w[l], w_uq[l], kv_norm_w[l], w_ukv[l],
                               cos_tab, sin_tab)
    o_mla = _attention(q, k, v, mla_out_norm_w[l])
    x1 = _out_projection(o_gdn.reshape(b * s, -1), o_mla.reshape(b * s, -1), x2, w_out[l])
    out = _ffn(x1, ffn_norm_w[l], w_gate[l], w_up[l], w_down[l], final_norm_w)
    return out.reshape(b, s, d)
```

```python
import functools
import math

import jax
import jax.numpy as jnp
from jax import lax
from jax.experimental import pallas as pl
from jax.experimental.pallas import tpu as pltpu

F32 = jnp.float32
BF16 = jnp.bfloat16

GDN_HEADS = 8
HEAD_DIM = 128
GDN_CONV = 4
GDN_CHUNK = 64
MLA_HEADS = 8
QK_NOPE = 128
QK_ROPE = 64
V_HEAD = 128
Q_LORA = 512
KV_LORA = 512
ROPE_THETA = 10000.0
EPS = 1e-6

LANES = 128
VMEM_LIMIT_BYTES = 56 * 2**20

INPROJ_TM = 1024
INPROJ_TN = 1792
GDN_T = 512
GDN_BLK = 2 * GDN_CHUNK
GDN_UNROLL = 2
MLA_TM = 1024
ATT_TK = 512
ATT_HEADS = 2
OUT_TM = 512
FFN_TM = 1024
FFN_TF = 512

QK_PAD = 256
SOFTMAX_SCALE = (QK_NOPE + QK_ROPE) ** -0.5
LOG2E = math.log2(math.e)


def _cparams(semantics):
    return pltpu.CompilerParams(dimension_semantics=semantics, vmem_limit_bytes=VMEM_LIMIT_BYTES)


def _sigmoid(x):
    return 1.0 / (1.0 + jnp.exp(-x))


def _silu(x):
    h = 0.5 * x
    return h + h * jnp.tanh(h)


def _softplus(x):
    return jnp.maximum(x, 0.0) + jnp.log1p(jnp.exp(-jnp.abs(x)))


def _dot(a, b):
    return jnp.dot(a, b, preferred_element_type=F32)


def _dot_nt(a, b):
    return lax.dot_general(a, b, (((1,), (1,)), ((), ())), preferred_element_type=F32)


def _dot_tn(a, b):
    return lax.dot_general(a, b, (((0,), (0,)), ((), ())), preferred_element_type=F32)


def _rope_kernel(pos_ref, invf_ref, cos_ref, sin_ref):
    half = QK_ROPE // 2
    per_row = LANES // half
    rows = pos_ref.shape[0]
    ang = pos_ref[...].astype(F32) * invf_ref[...]
    cos = jnp.cos(ang)
    sin = jnp.sin(ang)
    lane = lax.broadcasted_iota(jnp.int32, ang.shape, 1)
    for g in range(per_row):
        to_front = lambda x: pltpu.roll(x, LANES - g * half, axis=1) if g else x
        c1 = to_front(cos)
        s1 = to_front(sin)
        c2 = pltpu.roll(c1, half, axis=1)
        s2 = pltpu.roll(s1, half, axis=1)
        token_rows = pl.ds(g, rows, stride=per_row)
        cos_ref[token_rows, :] = jnp.where(lane < half, c1, jnp.where(lane < QK_ROPE, c2, 0.0))
        sin_ref[token_rows, :] = jnp.where(lane < half, -s1, jnp.where(lane < QK_ROPE, s2, 0.0))


def _rope_tables(positions):
    b, s = positions.shape
    m = b * s
    half = QK_ROPE // 2
    per_row = LANES // half
    rows = m // per_row
    blk = min(rows, 1024)
    inv_freq = ROPE_THETA ** (-jnp.arange(half, dtype=F32) / half)
    invf = jnp.tile(inv_freq, per_row).reshape(1, LANES)
    pos_rep = jnp.broadcast_to(positions.reshape(rows, per_row, 1), (rows, per_row, half)).reshape(rows, LANES)
    cos_tab, sin_tab = pl.pallas_call(
        _rope_kernel,
        grid=(rows // blk,),
        in_specs=[pl.BlockSpec((blk, LANES), lambda i: (i, 0)),
                  pl.BlockSpec((1, LANES), lambda i: (0, 0))],
        out_specs=[pl.BlockSpec((blk * per_row, LANES), lambda i: (i, 0)),
                   pl.BlockSpec((blk * per_row, LANES), lambda i: (i, 0))],
        out_shape=[jax.ShapeDtypeStruct((m, LANES), F32)] * 2,
        compiler_params=_cparams(("parallel",)),
        name="rope_table",
    )(pos_rep, invf)
    return cos_tab.reshape(b, s, LANES), sin_tab.reshape(b, s, LANES)


def _rotate_half_unsigned(x):
    return pltpu.roll(x, QK_ROPE // 2, axis=1)


def _with_x2_duplicate(w_rope, gap):
    return jnp.concatenate([w_rope, gap, w_rope[..., QK_ROPE // 2:]], axis=-1)


def _inproj_kernel(x_ref, nw_ref, w_ref, o_ref, small_ref, gates_ref, h_ref, *, nj, small_off):
    j = pl.program_id(1)

    @pl.when(j == 0)
    def _():
        x = x_ref[...]
        ms = jnp.mean(x * x, axis=-1, keepdims=True)
        h_ref[...] = (x * lax.rsqrt(ms + EPS) * nw_ref[...]).astype(BF16)

    acc = _dot(h_ref[...], w_ref[...])
    o_ref[...] = acc.astype(BF16)

    @pl.when(j == nj - 1)
    def _():
        small = acc[:, small_off:small_off + LANES]
        small_ref[...] = small
        for r in range(gates_ref.shape[0]):
            tile = small[r * LANES:(r + 1) * LANES, :].T
            gates_ref[r] = tile[QK_ROPE:QK_ROPE + 2 * GDN_HEADS, :]


def _in_projection(x2, norm_w, w_in):
    m, d = x2.shape
    conv_ch = 3 * GDN_HEADS * HEAD_DIM
    gdn_v = GDN_HEADS * HEAD_DIM
    w_in = w_in.astype(BF16)
    o = 0
    qkv = w_in[:, o:o + conv_ch]; o += conv_ch
    z = w_in[:, o:o + gdn_v]; o += gdn_v
    b_raw = w_in[:, o:o + GDN_HEADS]; o += GDN_HEADS
    a_raw = w_in[:, o:o + GDN_HEADS]; o += GDN_HEADS
    cq = w_in[:, o:o + Q_LORA]; o += Q_LORA
    ckv = w_in[:, o:o + KV_LORA]; o += KV_LORA
    kr = w_in[:, o:o + QK_ROPE]
    main_w = conv_ch + gdn_v + Q_LORA + KV_LORA
    tn = INPROJ_TN
    total = pl.cdiv(main_w + LANES, tn) * tn
    gates_w = jnp.concatenate([b_raw, a_raw], axis=1)
    gap = jnp.zeros((d, LANES - QK_ROPE - 2 * GDN_HEADS - QK_ROPE // 2), w_in.dtype)
    small_w = _with_x2_duplicate(kr, jnp.concatenate([gates_w, gap], axis=1))
    pad = jnp.zeros((d, total - main_w - LANES), w_in.dtype)
    w = jnp.concatenate([qkv, z, cq, ckv, small_w, pad], axis=1)
    nj = total // tn
    small_off = main_w - (nj - 1) * tn
    tm = INPROJ_TM
    kern = functools.partial(_inproj_kernel, nj=nj, small_off=small_off)
    return pl.pallas_call(
        kern,
        grid=(m // tm, nj),
        in_specs=[pl.BlockSpec((tm, d), lambda i, j: (i, 0)),
                  pl.BlockSpec((1, d), lambda i, j: (0, 0)),
                  pl.BlockSpec((d, tn), lambda i, j: (0, j))],
        out_specs=[pl.BlockSpec((tm, tn), lambda i, j: (i, j)),
                   pl.BlockSpec((tm, LANES), lambda i, j: (i, 0)),
                   pl.BlockSpec((tm // LANES, 2 * GDN_HEADS, LANES), lambda i, j: (i, 0, 0))],
        out_shape=[jax.ShapeDtypeStruct((m, total), BF16),
                   jax.ShapeDtypeStruct((m, LANES), F32),
                   jax.ShapeDtypeStruct((m // LANES, 2 * GDN_HEADS, LANES), F32)],
        scratch_shapes=[pltpu.VMEM((tm, d), BF16)],
        compiler_params=_cparams(("parallel", "arbitrary")),
        name="in_projection",
    )(x2, norm_w.reshape(1, d), w)


def _gdn_kernel(alog_ref, dtb_ref, q_ref, k_ref, v_ref, z_ref, a_ref, b_ref,
                cwq_ref, cwk_ref, cwv_ref, nw_ref, o_ref,
                state_ref, xq_ref, xk_ref, xv_ref, qn_ref, kn_ref, vv_ref, gc_ref, beta_ref):
    t = pl.program_id(1)
    T = q_ref.shape[1]
    nh = GDN_HEADS
    blk = GDN_BLK
    hist = blk
    c = GDN_CHUNK

    @pl.when(t == 0)
    def _():
        state_ref[...] = jnp.zeros_like(state_ref)
        zero_hist = jnp.zeros((hist, nh * HEAD_DIM), BF16)
        xq_ref[0:hist, :] = zero_hist
        xk_ref[0:hist, :] = zero_hist
        xv_ref[0:hist, :] = zero_hist

    taps = GDN_CONV - 1
    sel_row = lax.broadcasted_iota(jnp.int32, (taps * blk, hist + blk), 0)
    sel_col = lax.broadcasted_iota(jnp.int32, (taps * blk, hist + blk), 1)
    shift_sel = jnp.where(sel_col == sel_row % blk + hist - taps + sel_row // blk, 1.0, 0.0).astype(BF16)

    def conv_silu(xbuf, cw_ref, m):
        window = xbuf[m * blk:m * blk + hist + blk, :]
        w = cw_ref[...]
        shifted = _dot(shift_sel, window)
        y = window[hist:, :].astype(F32) * w[taps:taps + 1, :]
        for i in range(taps):
            y = y + shifted[i * blk:(i + 1) * blk, :] * w[i:i + 1, :]
        return _silu(y)

    for in_ref, xbuf in ((q_ref, xq_ref), (k_ref, xk_ref), (v_ref, xv_ref)):
        xbuf[hist:hist + T, :] = in_ref[0]
    for m in range(T // blk):
        rows = slice(m * blk, (m + 1) * blk)
        q = conv_silu(xq_ref, cwq_ref, m)
        k = conv_silu(xk_ref, cwk_ref, m)
        vv_ref[rows, :] = conv_silu(xv_ref, cwv_ref, m)
        for hd in range(nh):
            sl = slice(hd * HEAD_DIM, (hd + 1) * HEAD_DIM)
            qh = q[:, sl]
            kh = k[:, sl]
            qn_ref[rows, sl] = qh * (lax.rsqrt(jnp.sum(qh * qh, axis=-1, keepdims=True) + EPS) * (HEAD_DIM ** -0.5))
            kn_ref[rows, sl] = kh * lax.rsqrt(jnp.sum(kh * kh, axis=-1, keepdims=True) + EPS)
    for xbuf in (xq_ref, xk_ref, xv_ref):
        xbuf[0:hist, :] = xbuf[T:T + hist, :]

    neg_a = -jnp.exp(alog_ref[...])
    dtb = dtb_ref[...]
    lane = lax.broadcasted_iota(jnp.int32, (nh, LANES), 1)
    in_chunk = lane % c
    for r in range(T // LANES):
        tile = pl.ds(r * nh, nh)
        beta_ref[tile, :] = _sigmoid(b_ref[0, r])
        gc = neg_a * _softplus(a_ref[0, r] + dtb)
        s = 1
        while s < c:
            gc = gc + jnp.where(in_chunk >= s, pltpu.roll(gc, s, axis=1), 0.0)
            s *= 2
        gc_ref[tile, :] = gc

    row = lax.broadcasted_iota(jnp.int32, (blk, blk), 0)
    col = lax.broadcasted_iota(jnp.int32, (blk, blk), 1)
    same_chunk = (row // c) == (col // c)
    m_tril = jnp.logical_and(same_chunk, row >= col)
    m_strict = jnp.logical_and(same_chunk, row > col)
    eye = jnp.where(row == col, 1.0, 0.0).astype(F32)
    first = row < c

    heads = range(nh)
    sls = [slice(hd * HEAD_DIM, (hd + 1) * HEAD_DIM) for hd in heads]
    bf = lambda xs: [x.astype(BF16) for x in xs]

    unroll = GDN_UNROLL
    streams = [(sb, hd) for sb in range(unroll) for hd in heads]

    def block(rr, carry):
        rows = [pl.ds(pl.multiple_of((rr * unroll + sb) * blk, blk), blk) for sb in range(unroll)]
        gate = lambda sb, hd: pl.ds((rr * unroll + sb) * nh + hd, 1)
        qn = [qn_ref[rows[sb], sls[hd]] for sb, hd in streams]
        kn = [kn_ref[rows[sb], sls[hd]] for sb, hd in streams]
        v = [vv_ref[rows[sb], sls[hd]] for sb, hd in streams]
        g_row = [jnp.broadcast_to(gc_ref[gate(sb, hd), :], (blk, blk)) for sb, hd in streams]
        g_col = [x.T for x in g_row]
        b_col = [jnp.broadcast_to(beta_ref[gate(sb, hd), :], (blk, blk)).T for sb, hd in streams]
        decay = [jnp.exp(jnp.where(m_tril, gc_ - gr_, -jnp.inf)) for gc_, gr_ in zip(g_col, g_row)]
        kb = [a * b for a, b in zip(kn, b_col)]
        vb = [a * b for a, b in zip(v, b_col)]
        kn16 = bf(kn)
        kk = [_dot_nt(a, b) for a, b in zip(bf(kb), kn16)]
        qk = [_dot_nt(a, b) for a, b in zip(bf(qn), kn16)]
        a_mat = [jnp.where(m_strict, x * d, 0.0) for x, d in zip(kk, decay)]
        intra16 = bf([x * d for x, d in zip(qk, decay)])
        inv = [eye - a for a in a_mat]
        p16 = bf(a_mat)
        n = 2
        while n < c:
            p16 = bf([_dot(x, x) for x in p16])
            inv = [i + _dot(i16, x) for i, i16, x in zip(inv, bf(inv), p16)]
            n *= 2
        eg = [jnp.exp(x) for x in g_col]
        rhs = bf([jnp.concatenate([a, b * e], axis=1) for a, b, e in zip(vb, kb, eg)])
        uw = [_dot(i16, x) for i16, x in zip(bf(inv), rhs)]
        u = [x[:, :HEAD_DIM] for x in uw]
        w16 = bf([x[:, HEAD_DIM:] for x in uw])
        g_last1 = [jnp.broadcast_to(x[c - 1:c, :], (blk, blk)) for x in g_col]
        g_last2 = [jnp.broadcast_to(x[blk - 1:blk, :], (blk, blk)) for x in g_col]
        kdec16 = bf([k_ * jnp.exp(jnp.where(first, l1, l2) - gc_)
                     for k_, l1, l2, gc_ in zip(kn, g_last1, g_last2, g_col)])
        qg16 = bf([a * e for a, e in zip(qn, eg)])
        e1 = [jnp.exp(x) for x in g_last1]
        e2 = [jnp.exp(x) for x in g_last2]

        state = [state_ref[hd] for hd in heads]
        for sb in range(unroll):
            mine = lambda xs: xs[sb * nh:(sb + 1) * nh]
            u_, w_, q_, k_ = mine(u), mine(w16), mine(qg16), mine(kdec16)
            s016 = bf(state)
            vn1 = [a[0:c] - _dot(b[0:c], s_) for a, b, s_ in zip(u_, w_, s016)]
            oq1 = [_dot(a[0:c], s_) for a, s_ in zip(q_, s016)]
            s1 = [s_ * e_ + _dot_tn(a[0:c], x) for s_, e_, a, x in zip(state, mine(e1), k_, bf(vn1))]
            s116 = bf(s1)
            vn2 = [a[c:blk] - _dot(b[c:blk], s_) for a, b, s_ in zip(u_, w_, s116)]
            oq2 = [_dot(a[c:blk], s_) for a, s_ in zip(q_, s116)]
            state = [s_ * e_ + _dot_tn(a[c:blk], x) for s_, e_, a, x in zip(s1, mine(e2), k_, bf(vn2))]
            for hd in heads:
                vn = jnp.concatenate([vn1[hd], vn2[hd]], axis=0).astype(BF16)
                o = jnp.concatenate([oq1[hd], oq2[hd]], axis=0) + _dot(mine(intra16)[hd], vn)
                zz = z_ref[0, rows[sb], sls[hd]].astype(F32)
                y = o * lax.rsqrt(jnp.mean(o * o, axis=-1, keepdims=True) + EPS) * nw
                o_ref[0, rows[sb], sls[hd]] = (y * _silu(zz)).astype(o_ref.dtype)
        for hd in heads:
            state_ref[hd] = state[hd]
        return carry

    nw = nw_ref[...]
    lax.fori_loop(0, T // (blk * unroll), block, 0)


def _gated_delta(proj3, gates, conv_w, a_log, dt_bias, norm_w):
    b, s, _ = proj3.shape
    hh = GDN_HEADS
    T = GDN_T
    tiles = T // LANES
    width = hh * HEAD_DIM
    gates4 = gates.reshape(b, s // LANES, 2 * hh, LANES)
    lane_bcast = lambda p: jnp.broadcast_to(p.reshape(hh, 1), (hh, LANES))
    col_spec = lambda j: pl.BlockSpec((1, T, width), lambda bi, ti: (bi, ti, j))
    gate_spec = lambda j: pl.BlockSpec((1, tiles, hh, LANES), lambda bi, ti: (bi, ti, j, 0))
    cw_spec = lambda j: pl.BlockSpec((GDN_CONV, width), lambda bi, ti: (0, j))
    head_spec = pl.BlockSpec((hh, LANES), lambda bi, ti: (0, 0))
    buf = pltpu.VMEM((T + GDN_BLK, width), BF16)
    full = pltpu.VMEM((T, width), F32)
    gate_buf = pltpu.VMEM((tiles * hh, LANES), F32)
    return pl.pallas_call(
        _gdn_kernel,
        grid=(b, s // T),
        in_specs=[head_spec, head_spec,
                  col_spec(0), col_spec(1), col_spec(2), col_spec(3),
                  gate_spec(1), gate_spec(0),
                  cw_spec(0), cw_spec(1), cw_spec(2),
                  pl.BlockSpec((1, HEAD_DIM), lambda bi, ti: (0, 0))],
        out_specs=pl.BlockSpec((1, T, width), lambda bi, ti: (bi, ti, 0)),
        out_shape=jax.ShapeDtypeStruct((b, s, width), BF16),
        scratch_shapes=[pltpu.VMEM((hh, HEAD_DIM, HEAD_DIM), F32), buf, buf, buf, full, full, full,
                        gate_buf, gate_buf],
        compiler_params=_cparams(("parallel", "arbitrary")),
        name="gated_delta",
    )(lane_bcast(a_log), lane_bcast(dt_bias), proj3, proj3, proj3, proj3, gates4, gates4,
      conv_w, conv_w, conv_w, norm_w.reshape(1, HEAD_DIM))


def _rms_rows(x, w):
    ms = jnp.mean(x * x, axis=-1, keepdims=True)
    return x * lax.rsqrt(ms + EPS) * w


def _qproj_kernel(c_ref, nw_ref, w_ref, cos_ref, sin_ref, o_ref):
    hq = _rms_rows(c_ref[0].astype(F32), nw_ref[...]).astype(BF16)
    res = _dot(hq, w_ref[...])
    cos = cos_ref[0]
    sin = sin_ref[0]
    scale = SOFTMAX_SCALE * LOG2E
    for hd in range(MLA_HEADS):
        base = hd * QK_PAD
        nope = res[:, base:base + QK_NOPE]
        rp = res[:, base + QK_NOPE:base + QK_PAD]
        rot = rp * cos + _rotate_half_unsigned(rp) * sin
        o_ref[0, hd, :, 0:QK_NOPE] = (nope * scale).astype(BF16)
        o_ref[0, hd, :, QK_NOPE:QK_PAD] = (rot * scale).astype(BF16)


def _kvproj_kernel(c_ref, small_ref, nw_ref, w_ref, cos_ref, sin_ref, k_ref, v_ref):
    hkv = _rms_rows(c_ref[0].astype(F32), nw_ref[...]).astype(BF16)
    res = _dot(hkv, w_ref[...])
    cos = cos_ref[0]
    sin = sin_ref[0]
    kr = small_ref[0]
    kr16 = (kr * cos + _rotate_half_unsigned(kr) * sin).astype(BF16)
    nk = MLA_HEADS * QK_NOPE
    for hd in range(MLA_HEADS):
        k_ref[0, hd, :, 0:QK_NOPE] = res[:, hd * QK_NOPE:(hd + 1) * QK_NOPE].astype(BF16)
        k_ref[0, hd, :, QK_NOPE:QK_PAD] = kr16
        v_ref[0, hd] = res[:, nk + hd * V_HEAD:nk + (hd + 1) * V_HEAD].astype(BF16)


def _mla_projections(proj3, small3, q_norm_w, w_uq, kv_norm_w, w_ukv, cos_tab, sin_tab):
    b, s, _ = proj3.shape
    hh = MLA_HEADS
    tm = MLA_TM
    main_cq = (3 * GDN_HEADS * HEAD_DIM + GDN_HEADS * HEAD_DIM) // Q_LORA
    main_ckv = main_cq + 1
    wq = w_uq.reshape(Q_LORA, hh, QK_NOPE + QK_ROPE)
    gap = jnp.zeros((Q_LORA, hh, QK_PAD - QK_NOPE - QK_ROPE - QK_ROPE // 2), w_uq.dtype)
    wq = jnp.concatenate([wq[:, :, :QK_NOPE], _with_x2_duplicate(wq[:, :, QK_NOPE:], gap)], axis=2)
    wq = wq.reshape(Q_LORA, hh * QK_PAD).astype(BF16)
    wkv = w_ukv.reshape(KV_LORA, hh, QK_NOPE + V_HEAD)
    wkv = jnp.concatenate([wkv[:, :, :QK_NOPE].reshape(KV_LORA, hh * QK_NOPE),
                           wkv[:, :, QK_NOPE:].reshape(KV_LORA, hh * V_HEAD)], axis=1).astype(BF16)
    tab_spec = pl.BlockSpec((1, tm, LANES), lambda bi, ti: (bi, ti, 0))
    q = pl.pallas_call(
        _qproj_kernel,
        grid=(b, s // tm),
        in_specs=[pl.BlockSpec((1, tm, Q_LORA), lambda bi, ti: (bi, ti, main_cq)),
                  pl.BlockSpec((1, Q_LORA), lambda bi, ti: (0, 0)),
                  pl.BlockSpec((Q_LORA, hh * QK_PAD), lambda bi, ti: (0, 0)),
                  tab_spec, tab_spec],
        out_specs=pl.BlockSpec((1, hh, tm, QK_PAD), lambda bi, ti: (bi, 0, ti, 0)),
        out_shape=jax.ShapeDtypeStruct((b, hh, s, QK_PAD), BF16),
        compiler_params=_cparams(("parallel", "parallel")),
        name="mla_q_proj",
    )(proj3, q_norm_w.reshape(1, Q_LORA), wq, cos_tab, sin_tab)
    k, v = pl.pallas_call(
        _kvproj_kernel,
        grid=(b, s // tm),
        in_specs=[pl.BlockSpec((1, tm, KV_LORA), lambda bi, ti: (bi, ti, main_ckv)),
                  tab_spec,
                  pl.BlockSpec((1, KV_LORA), lambda bi, ti: (0, 0)),
                  pl.BlockSpec((KV_LORA, hh * (QK_NOPE + V_HEAD)), lambda bi, ti: (0, 0)),
                  tab_spec, tab_spec],
        out_specs=[pl.BlockSpec((1, hh, tm, QK_PAD), lambda bi, ti: (bi, 0, ti, 0)),
                   pl.BlockSpec((1, hh, tm, V_HEAD), lambda bi, ti: (bi, 0, ti, 0))],
        out_shape=[jax.ShapeDtypeStruct((b, hh, s, QK_PAD), BF16),
                   jax.ShapeDtypeStruct((b, hh, s, V_HEAD), BF16)],
        compiler_params=_cparams(("parallel", "parallel")),
        name="mla_kv_proj",
    )(proj3, small3, kv_norm_w.reshape(1, KV_LORA), wkv, cos_tab, sin_tab)
    return q, k, v


def _attn_kernel(q_ref, k_ref, v_ref, nw_ref, o_ref, *scratch):
    nh = ATT_HEADS
    tk = ATT_TK
    streams = [(hd, half) for hd in range(nh) for half in range(2)]
    ns = len(streams)
    s_refs, (m_ref, acc_ref) = scratch[:ns], scratch[ns:]
    i = pl.program_id(2)
    kb = 2 * tk
    q = [q_ref[0, hd, half * tk:(half + 1) * tk, :] for hd, half in streams]
    m_ref[...] = jnp.full(m_ref.shape, -jnp.inf, F32)
    acc_ref[...] = jnp.zeros(acc_ref.shape, F32)

    def scores(st, start, n):
        return _dot_nt(q[st], k_ref[0, streams[st][0], pl.ds(start, n), :])

    def update(st, sc, start, n, visible_past=None):
        if visible_past is not None:
            qpos = lax.broadcasted_iota(jnp.int32, sc.shape, 0)
            kpos = lax.broadcasted_iota(jnp.int32, sc.shape, 1)
            sc = jnp.where(kpos <= qpos + visible_past, sc, -jnp.inf)
        vb = v_ref[0, streams[st][0], pl.ds(start, n), :]
        vb = jnp.concatenate([vb, jnp.ones((n, V_HEAD), BF16)], axis=1)
        m_prev = m_ref[st]
        m_new = jnp.maximum(m_prev, jnp.max(sc, axis=-1, keepdims=True))
        p = jnp.exp2(sc - jnp.concatenate([m_new] * (n // LANES), axis=1))
        alpha = jnp.exp2(m_prev - m_new)
        acc_ref[st] = jnp.concatenate([alpha, alpha], axis=1) * acc_ref[st] + _dot(p.astype(BF16), vb)
        m_ref[st] = m_new

    every = range(ns)

    def body(jj, carry):
        start = pl.multiple_of(jj * kb, kb)
        for st in every:
            s_refs[st][...] = scores(st, start, kb)
        for st in every:
            update(st, s_refs[st][...], start, kb)
        return carry

    lax.fori_loop(0, i, body, 0)

    start = pl.multiple_of(i * kb, kb)
    for st in every:
        if streams[st][1] == 0:
            s_refs[st][:, 0:tk] = scores(st, start, tk)
        else:
            s_refs[st][...] = scores(st, start, kb)
    for st in every:
        if streams[st][1] == 0:
            update(st, s_refs[st][:, 0:tk], start, tk, visible_past=0)
        else:
            update(st, s_refs[st][...], start, kb, visible_past=tk)

    nw = nw_ref[...]
    for st, (hd, half) in enumerate(streams):
        acc = acc_ref[st]
        o = acc[:, :V_HEAD] / acc[:, V_HEAD:]
        y = o * lax.rsqrt(jnp.mean(o * o, axis=-1, keepdims=True) + EPS) * nw
        o_ref[0, half * tk:(half + 1) * tk, hd * V_HEAD:(hd + 1) * V_HEAD] = y.astype(o_ref.dtype)


def _attention(q, k, v, norm_w):
    b, hh, s, _ = q.shape
    tk = ATT_TK
    tq = 2 * tk
    nh = ATT_HEADS
    ns = 2 * nh
    score_buf = pltpu.VMEM((tk, 2 * tk), F32)
    return pl.pallas_call(
        _attn_kernel,
        grid=(b, hh // nh, s // tq),
        in_specs=[pl.BlockSpec((1, nh, tq, QK_PAD), lambda bi, hi, qi: (bi, hi, qi, 0)),
                  pl.BlockSpec((1, nh, s, QK_PAD), lambda bi, hi, qi: (bi, hi, 0, 0)),
                  pl.BlockSpec((1, nh, s, V_HEAD), lambda bi, hi, qi: (bi, hi, 0, 0)),
                  pl.BlockSpec((1, V_HEAD), lambda bi, hi, qi: (0, 0))],
        out_specs=pl.BlockSpec((1, tq, nh * V_HEAD), lambda bi, hi, qi: (bi, qi, hi)),
        out_shape=jax.ShapeDtypeStruct((b, s, hh * V_HEAD), BF16),
        scratch_shapes=[score_buf] * ns + [pltpu.VMEM((ns, tk, V_HEAD), F32),
                                            pltpu.VMEM((ns, tk, 2 * V_HEAD), F32)],
        compiler_params=_cparams(("parallel", "parallel", "arbitrary")),
        name="mla_attention",
    )(q, k, v, norm_w.reshape(1, V_HEAD))


def _outproj_kernel(og_ref, om_ref, x_ref, wg_ref, wm_ref, o_ref):
    o_ref[...] = x_ref[...] + _dot(og_ref[...], wg_ref[...]) + _dot(om_ref[...], wm_ref[...])


def _out_projection(o_gdn, o_mla, x2, w_out):
    m, d = x2.shape
    kg = o_gdn.shape[1]
    km = o_mla.shape[1]
    tm = OUT_TM
    w16 = w_out.astype(BF16)
    return pl.pallas_call(
        _outproj_kernel,
        grid=(m // tm,),
        in_specs=[pl.BlockSpec((tm, kg), lambda i: (i, 0)),
                  pl.BlockSpec((tm, km), lambda i: (i, 0)),
                  pl.BlockSpec((tm, d), lambda i: (i, 0)),
                  pl.BlockSpec((kg, d), lambda i: (0, 0)),
                  pl.BlockSpec((km, d), lambda i: (0, 0))],
        out_specs=pl.BlockSpec((tm, d), lambda i: (i, 0)),
        out_shape=jax.ShapeDtypeStruct((m, d), F32),
        compiler_params=_cparams(("parallel",)),
        name="out_projection",
    )(o_gdn, o_mla, x2, w16[:kg], w16[kg:])


def _ffn_kernel(x_ref, nw_ref, wg_ref, wu_ref, wd_ref, fw_ref, o_ref, h_ref, *, nj):
    j = pl.program_id(1)

    tm = x_ref.shape[0]

    @pl.when(j == 0)
    def _():
        x = x_ref[...]
        h_ref[...] = _rms_rows(x, nw_ref[...]).astype(BF16)
        o_ref[...] = x

    halves = [slice(0, tm // 2), slice(tm // 2, tm)]
    wg = wg_ref[...]
    wu = wu_ref[...]
    gu = [(_dot(h_ref[rows, :], wg), _dot(h_ref[rows, :], wu)) for rows in halves]
    act = [(_silu(g) * u).astype(BF16) for g, u in gu]
    wd = wd_ref[...]
    for rows, a in zip(halves, act):
        o_ref[rows, :] += _dot(a, wd)

    @pl.when(j == nj - 1)
    def _():
        o_ref[...] = _rms_rows(o_ref[...], fw_ref[...])


def _ffn(x1, norm_w, w_gate, w_up, w_down, final_w):
    m, d = x1.shape
    dff = w_gate.shape[1]
    tm = FFN_TM
    tf = FFN_TF
    nj = dff // tf
    kern = functools.partial(_ffn_kernel, nj=nj)
    return pl.pallas_call(
        kern,
        grid=(m // tm, nj),
        in_specs=[pl.BlockSpec((tm, d), lambda i, j: (i, 0)),
                  pl.BlockSpec((1, d), lambda i, j: (0, 0)),
                  pl.BlockSpec((d, tf), lambda i, j: (0, j)),
                  pl.BlockSpec((d, tf), lambda i, j: (0, j)),
                  pl.BlockSpec((tf, d), lambda i, j: (j, 0)),
                  pl.BlockSpec((1, d), lambda i, j: (0, 0))],
        out_specs=pl.BlockSpec((tm, d), lambda i, j: (i, 0)),
        out_shape=jax.ShapeDtypeStruct((m, d), F32),
        scratch_shapes=[pltpu.VMEM((tm, d), BF16)],
        compiler_params=_cparams(("parallel", "arbitrary")),
        name="swiglu_ffn",
    )(x1, norm_w.reshape(1, d), w_gate.astype(BF16), w_up.astype(BF16), w_down.astype(BF16),
      final_w.reshape(1, d))


def kernel(x, positions, attn_norm_w, w_in, conv_w, a_log, dt_bias, gdn_norm_w, q_norm_w, w_uq,
           kv_norm_w, w_ukv, mla_out_norm_w, w_out, ffn_norm_w, w_gate, w_up, w_down, final_norm_w):
    b, s, d = x.shape
    assert w_in.shape[0] == 1, "the final rmsnorm is fused into the (single) layer's FFN kernel"
    l = 0
    cos_tab, sin_tab = _rope_tables(positions)
    x2 = x.reshape(b * s, d)
    proj, small, gates = _in_projection(x2, attn_norm_w[l], w_in[l])
    proj3 = proj.reshape(b, s, proj.shape[1])
    small3 = small.reshape(b, s, LANES)
    o_gdn = _gated_delta(proj3, gates, conv_w[l], a_log[l], dt_bias[l], gdn_norm_w[l])
    q, k, v = _mla_projections(proj3, small3, q_norm_w[l], w_uq[l], kv_norm_w[l], w_ukv[l],
                               cos_tab, sin_tab)
    o_mla = _attention(q, k, v, mla_out_norm_w[l])
    x1 = _out_projection(o_gdn.reshape(b * s, -1), o_mla.reshape(b * s, -1), x2, w_out[l])
    out = _ffn(x1, ffn_norm_w[l], w_gate[l], w_up[l], w_down[l], final_norm_w)
    return out.reshape(b, s, d)
```

```python
import functools
import math

import jax
import jax.numpy as jnp
from jax import lax
from jax.experimental import pallas as pl
from jax.experimental.pallas import tpu as pltpu

F32 = jnp.float32
BF16 = jnp.bfloat16

GDN_HEADS = 8
HEAD_DIM = 128
GDN_CONV = 4
GDN_CHUNK = 64
MLA_HEADS = 8
QK_NOPE = 128
QK_ROPE = 64
V_HEAD = 128
Q_LORA = 512
KV_LORA = 512
ROPE_THETA = 10000.0
EPS = 1e-6

LANES = 128
VMEM_LIMIT_BYTES = 56 * 2**20

INPROJ_TM = 1024
INPROJ_TN = 1792
GDN_T = 512
GDN_BLK = 2 * GDN_CHUNK
GDN_UNROLL = 2
MLA_TM = 1024
ATT_TK = 512
ATT_HEADS = 2
OUT_TM = 512
FFN_TM = 1024
FFN_TF = 512

QK_PAD = 256
SOFTMAX_SCALE = (QK_NOPE + QK_ROPE) ** -0.5
LOG2E = math.log2(math.e)


def _cparams(semantics):
    return pltpu.CompilerParams(dimension_semantics=semantics, vmem_limit_bytes=VMEM_LIMIT_BYTES)


def _sigmoid(x):
    return 1.0 / (1.0 + jnp.exp(-x))


def _silu(x):
    h = 0.5 * x
    return h + h * jnp.tanh(h)


def _softplus(x):
    return jnp.maximum(x, 0.0) + jnp.log1p(jnp.exp(-jnp.abs(x)))


def _dot(a, b):
    return jnp.dot(a, b, preferred_element_type=F32)


def _dot_nt(a, b):
    return lax.dot_general(a, b, (((1,), (1,)), ((), ())), preferred_element_type=F32)


def _dot_tn(a, b):
    return lax.dot_general(a, b, (((0,), (0,)), ((), ())), preferred_element_type=F32)


def _rope_kernel(pos_ref, invf_ref, cos_ref, sin_ref):
    half = QK_ROPE // 2
    per_row = LANES // half
    rows = pos_ref.shape[0]
    ang = pos_ref[...].astype(F32) * invf_ref[...]
    cos = jnp.cos(ang)
    sin = jnp.sin(ang)
    lane = lax.broadcasted_iota(jnp.int32, ang.shape, 1)
    for g in range(per_row):
        to_front = lambda x: pltpu.roll(x, LANES - g * half, axis=1) if g else x
        c1 = to_front(cos)
        s1 = to_front(sin)
        c2 = pltpu.roll(c1, half, axis=1)
        s2 = pltpu.roll(s1, half, axis=1)
        token_rows = pl.ds(g, rows, stride=per_row)
        cos_ref[token_rows, :] = jnp.where(lane < half, c1, jnp.where(lane < QK_ROPE, c2, 0.0))
        sin_ref[token_rows, :] = jnp.where(lane < half, -s1, jnp.where(lane < QK_ROPE, s2, 0.0))


def _rope_tables(positions):
    b, s = positions.shape
    m = b * s
    half = QK_ROPE // 2
    per_row = LANES // half
    rows = m // per_row
    blk = min(rows, 1024)
    inv_freq = ROPE_THETA ** (-jnp.arange(half, dtype=F32) / half)
    invf = jnp.tile(inv_freq, per_row).reshape(1, LANES)
    pos_rep = jnp.broadcast_to(positions.reshape(rows, per_row, 1), (rows, per_row, half)).reshape(rows, LANES)
    cos_tab, sin_tab = pl.pallas_call(
        _rope_kernel,
        grid=(rows // blk,),
        in_specs=[pl.BlockSpec((blk, LANES), lambda i: (i, 0)),
                  pl.BlockSpec((1, LANES), lambda i: (0, 0))],
        out_specs=[pl.BlockSpec((blk * per_row, LANES), lambda i: (i, 0)),
                   pl.BlockSpec((blk * per_row, LANES), lambda i: (i, 0))],
        out_shape=[jax.ShapeDtypeStruct((m, LANES), F32)] * 2,
        compiler_params=_cparams(("parallel",)),
        name="rope_table",
    )(pos_rep, invf)
    return cos_tab.reshape(b, s, LANES), sin_tab.reshape(b, s, LANES)


def _rotate_half_unsigned(x):
    return pltpu.roll(x, QK_ROPE // 2, axis=1)


def _with_x2_duplicate(w_rope, gap):
    return jnp.concatenate([w_rope, gap, w_rope[..., QK_ROPE // 2:]], axis=-1)


def _inproj_kernel(x_ref, nw_ref, w_ref, o_ref, small_ref, gates_ref, h_ref, *, nj, small_off):
    j = pl.program_id(1)

    @pl.when(j == 0)
    def _():
        x = x_ref[...]
        ms = jnp.mean(x * x, axis=-1, keepdims=True)
        h_ref[...] = (x * lax.rsqrt(ms + EPS) * nw_ref[...]).astype(BF16)

    acc = _dot(h_ref[...], w_ref[...])
    o_ref[...] = acc.astype(BF16)

    @pl.when(j == nj - 1)
    def _():
        small = acc[:, small_off:small_off + LANES]
        small_ref[...] = small
        for r in range(gates_ref.shape[0]):
            tile = small[r * LANES:(r + 1) * LANES, :].T
            gates_ref[r] = tile[QK_ROPE:QK_ROPE + 2 * GDN_HEADS, :]


def _in_projection(x2, norm_w, w_in):
    m, d = x2.shape
    conv_ch = 3 * GDN_HEADS * HEAD_DIM
    gdn_v = GDN_HEADS * HEAD_DIM
    w_in = w_in.astype(BF16)
    o = 0
    qkv = w_in[:, o:o + conv_ch]; o += conv_ch
    z = w_in[:, o:o + gdn_v]; o += gdn_v
    b_raw = w_in[:, o:o + GDN_HEADS]; o += GDN_HEADS
    a_raw = w_in[:, o:o + GDN_HEADS]; o += GDN_HEADS
    cq = w_in[:, o:o + Q_LORA]; o += Q_LORA
    ckv = w_in[:, o:o + KV_LORA]; o += KV_LORA
    kr = w_in[:, o:o + QK_ROPE]
    main_w = conv_ch + gdn_v + Q_LORA + KV_LORA
    tn = INPROJ_TN
    total = pl.cdiv(main_w + LANES, tn) * tn
    gates_w = jnp.concatenate([b_raw, a_raw], axis=1)
    gap = jnp.zeros((d, LANES - QK_ROPE - 2 * GDN_HEADS - QK_ROPE // 2), w_in.dtype)
    small_w = _with_x2_duplicate(kr, jnp.concatenate([gates_w, gap], axis=1))
    pad = jnp.zeros((d, total - main_w - LANES), w_in.dtype)
    w = jnp.concatenate([qkv, z, cq, ckv, small_w, pad], axis=1)
    nj = total // tn
    small_off = main_w - (nj - 1) * tn
    tm = INPROJ_TM
    kern = functools.partial(_inproj_kernel, nj=nj, small_off=small_off)
    return pl.pallas_call(
        kern,
        grid=(m // tm, nj),
        in_specs=[pl.BlockSpec((tm, d), lambda i, j: (i, 0)),
                  pl.BlockSpec((1, d), lambda i, j: (0, 0)),
                  pl.BlockSpec((d, tn), lambda i, j: (0, j))],
        out_specs=[pl.BlockSpec((tm, tn), lambda i, j: (i, j)),
                   pl.BlockSpec((tm, LANES), lambda i, j: (i, 0)),
                   pl.BlockSpec((tm // LANES, 2 * GDN_HEADS, LANES), lambda i, j: (i, 0, 0))],
        out_shape=[jax.ShapeDtypeStruct((m, total), BF16),
                   jax.ShapeDtypeStruct((m, LANES), F32),
                   jax.ShapeDtypeStruct((m // LANES, 2 * GDN_HEADS, LANES), F32)],
        scratch_shapes=[pltpu.VMEM((tm, d), BF16)],
        compiler_params=_cparams(("parallel", "arbitrary")),
        name="in_projection",
    )(x2, norm_w.reshape(1, d), w)


def _gdn_kernel(alog_ref, dtb_ref, q_ref, k_ref, v_ref, z_ref, a_ref, b_ref,
                cwq_ref, cwk_ref, cwv_ref, nw_ref, o_ref,
                state_ref, xq_ref, xk_ref, xv_ref, qn_ref, kn_ref, vv_ref, gc_ref, beta_ref):
    t = pl.program_id(1)
    T = q_ref.shape[1]
    nh = GDN_HEADS
    blk = GDN_BLK
    hist = blk
    c = GDN_CHUNK

    @pl.when(t == 0)
    def _():
        state_ref[...] = jnp.zeros_like(state_ref)
        zero_hist = jnp.zeros((hist, nh * HEAD_DIM), BF16)
        xq_ref[0:hist, :] = zero_hist
        xk_ref[0:hist, :] = zero_hist
        xv_ref[0:hist, :] = zero_hist

    taps = GDN_CONV - 1
    sel_row = lax.broadcasted_iota(jnp.int32, (taps * blk, hist + blk), 0)
    sel_col = lax.broadcasted_iota(jnp.int32, (taps * blk, hist + blk), 1)
    shift_sel = jnp.where(sel_col == sel_row % blk + hist - taps + sel_row // blk, 1.0, 0.0).astype(BF16)

    def conv_silu(xbuf, cw_ref, m):
        window = xbuf[m * blk:m * blk + hist + blk, :]
        w = cw_ref[...]
        shifted = _dot(shift_sel, window)
        y = window[hist:, :].astype(F32) * w[taps:taps + 1, :]
        for i in range(taps):
            y = y + shifted[i * blk:(i + 1) * blk, :] * w[i:i + 1, :]
        return _silu(y)

    for in_ref, xbuf in ((q_ref, xq_ref), (k_ref, xk_ref), (v_ref, xv_ref)):
        xbuf[hist:hist + T, :] = in_ref[0]
    for m in range(T // blk):
        rows = slice(m * blk, (m + 1) * blk)
        q = conv_silu(xq_ref, cwq_ref, m)
        k = conv_silu(xk_ref, cwk_ref, m)
        vv_ref[rows, :] = conv_silu(xv_ref, cwv_ref, m)
        for hd in range(nh):
            sl = slice(hd * HEAD_DIM, (hd + 1) * HEAD_DIM)
            qh = q[:, sl]
            kh = k[:, sl]
            qn_ref[rows, sl] = qh * (lax.rsqrt(jnp.sum(qh * qh, axis=-1, keepdims=True) + EPS) * (HEAD_DIM ** -0.5))
            kn_ref[rows, sl] = kh * lax.rsqrt(jnp.sum(kh * kh, axis=-1, keepdims=True) + EPS)
    for xbuf in (xq_ref, xk_ref, xv_ref):
        xbuf[0:hist, :] = xbuf[T:T + hist, :]

    neg_a = -jnp.exp(alog_ref[...])
    dtb = dtb_ref[...]
    lane = lax.broadcasted_iota(jnp.int32, (nh, LANES), 1)
    in_chunk = lane % c
    for r in range(T // LANES):
        tile = pl.ds(r * nh, nh)
        beta_ref[tile, :] = _sigmoid(b_ref[0, r])
        gc = neg_a * _softplus(a_ref[0, r] + dtb)
        s = 1
        while s < c:
            gc = gc + jnp.where(in_chunk >= s, pltpu.roll(gc, s, axis=1), 0.0)
            s *= 2
        gc_ref[tile, :] = gc

    row = lax.broadcasted_iota(jnp.int32, (blk, blk), 0)
    col = lax.broadcasted_iota(jnp.int32, (blk, blk), 1)
    same_chunk = (row // c) == (col // c)
    m_tril = jnp.logical_and(same_chunk, row >= col)
    m_strict = jnp.logical_and(same_chunk, row > col)
    eye = jnp.where(row == col, 1.0, 0.0).astype(F32)
    first = row < c

    heads = range(nh)
    sls = [slice(hd * HEAD_DIM, (hd + 1) * HEAD_DIM) for hd in heads]
    bf = lambda xs: [x.astype(BF16) for x in xs]

    unroll = GDN_UNROLL
    streams = [(sb, hd) for sb in range(unroll) for hd in heads]

    def block(rr, carry):
        rows = [pl.ds(pl.multiple_of((rr * unroll + sb) * blk, blk), blk) for sb in range(unroll)]
        gate = lambda sb, hd: pl.ds((rr * unroll + sb) * nh + hd, 1)
        qn = [qn_ref[rows[sb], sls[hd]] for sb, hd in streams]
        kn = [kn_ref[rows[sb], sls[hd]] for sb, hd in streams]
        v = [vv_ref[rows[sb], sls[hd]] for sb, hd in streams]
        g_row = [jnp.broadcast_to(gc_ref[gate(sb, hd), :], (blk, blk)) for sb, hd in streams]
        g_col = [x.T for x in g_row]
        b_col = [jnp.broadcast_to(beta_ref[gate(sb, hd), :], (blk, blk)).T for sb, hd in streams]
        decay = [jnp.exp(jnp.where(m_tril, gc_ - gr_, -jnp.inf)) for gc_, gr_ in zip(g_col, g_row)]
        kb = [a * b for a, b in zip(kn, b_col)]
        vb = [a * b for a, b in zip(v, b_col)]
        kn16 = bf(kn)
        kk = [_dot_nt(a, b) for a, b in zip(bf(kb), kn16)]
        qk = [_dot_nt(a, b) for a, b in zip(bf(qn), kn16)]
        a_mat = [jnp.where(m_strict, x * d, 0.0) for x, d in zip(kk, decay)]
        intra16 = bf([x * d for x, d in zip(qk, decay)])
        inv = [eye - a for a in a_mat]
        p16 = bf(a_mat)
        n = 2
        while n < c:
            p16 = bf([_dot(x, x) for x in p16])
            inv = [i + _dot(i16, x) for i, i16, x in zip(inv, bf(inv), p16)]
            n *= 2
        eg = [jnp.exp(x) for x in g_col]
        rhs = bf([jnp.concatenate([a, b * e], axis=1) for a, b, e in zip(vb, kb, eg)])
        uw = [_dot(i16, x) for i16, x in zip(bf(inv), rhs)]
        u = [x[:, :HEAD_DIM] for x in uw]
        w16 = bf([x[:, HEAD_DIM:] for x in uw])
        g_last1 = [jnp.broadcast_to(x[c - 1:c, :], (blk, blk)) for x in g_col]
        g_last2 = [jnp.broadcast_to(x[blk - 1:blk, :], (blk, blk)) for x in g_col]
        kdec16 = bf([k_ * jnp.exp(jnp.where(first, l1, l2) - gc_)
                     for k_, l1, l2, gc_ in zip(kn, g_last1, g_last2, g_col)])
        qg16 = bf([a * e for a, e in zip(qn, eg)])
        e1 = [jnp.exp(x) for x in g_last1]
        e2 = [jnp.exp(x) for x in g_last2]

        state = [state_ref[hd] for hd in heads]
        for sb in range(unroll):
            mine = lambda xs: xs[sb * nh:(sb + 1) * nh]
            u_, w_, q_, k_ = mine(u), mine(w16), mine(qg16), mine(kdec16)
            s016 = bf(state)
            vn1 = [a[0:c] - _dot(b[0:c], s_) for a, b, s_ in zip(u_, w_, s016)]
            oq1 = [_dot(a[0:c], s_) for a, s_ in zip(q_, s016)]
            s1 = [s_ * e_ + _dot_tn(a[0:c], x) for s_, e_, a, x in zip(state, mine(e1), k_, bf(vn1))]
            s116 = bf(s1)
            vn2 = [a[c:blk] - _dot(b[c:blk], s_) for a, b, s_ in zip(u_, w_, s116)]
            oq2 = [_dot(a[c:blk], s_) for a, s_ in zip(q_, s116)]
            state = [s_ * e_ + _dot_tn(a[c:blk], x) for s_, e_, a, x in zip(s1, mine(e2), k_, bf(vn2))]
            for hd in heads:
                vn = jnp.concatenate([vn1[hd], vn2[hd]], axis=0).astype(BF16)
                o = jnp.concatenate([oq1[hd], oq2[hd]], axis=0) + _dot(mine(intra16)[hd], vn)
                zz = z_ref[0, rows[sb], sls[hd]].astype(F32)
                y = o * lax.rsqrt(jnp.mean(o * o, axis=-1, keepdims=True) + EPS) * nw
                o_ref[0, rows[sb], sls[hd]] = (y * _silu(zz)).astype(o_ref.dtype)
        for hd in heads:
            state_ref[hd] = state[hd]
        return carry

    nw = nw_ref[...]
    lax.fori_loop(0, T // (blk * unroll), block, 0)


def _gated_delta(proj3, gates, conv_w, a_log, dt_bias, norm_w):
    b, s, _ = proj3.shape
    hh = GDN_HEADS
    T = GDN_T
    tiles = T // LANES
    width = hh * HEAD_DIM
    gates4 = gates.reshape(b, s // LANES, 2 * hh, LANES)
    lane_bcast = lambda p: jnp.broadcast_to(p.reshape(hh, 1), (hh, LANES))
    col_spec = lambda j: pl.BlockSpec((1, T, width), lambda bi, ti: (bi, ti, j))
    gate_spec = lambda j: pl.BlockSpec((1, tiles, hh, LANES), lambda bi, ti: (bi, ti, j, 0))
    cw_spec = lambda j: pl.BlockSpec((GDN_CONV, width), lambda bi, ti: (0, j))
    head_spec = pl.BlockSpec((hh, LANES), lambda bi, ti: (0, 0))
    buf = pltpu.VMEM((T + GDN_BLK, width), BF16)
    full = pltpu.VMEM((T, width), F32)
    gate_buf = pltpu.VMEM((tiles * hh, LANES), F32)
    return pl.pallas_call(
        _gdn_kernel,
        grid=(b, s // T),
        in_specs=[head_spec, head_spec,
                  col_spec(0), col_spec(1), col_spec(2), col_spec(3),
                  gate_spec(1), gate_spec(0),
                  cw_spec(0), cw_spec(1), cw_spec(2),
                  pl.BlockSpec((1, HEAD_DIM), lambda bi, ti: (0, 0))],
        out_specs=pl.BlockSpec((1, T, width), lambda bi, ti: (bi, ti, 0)),
        out_shape=jax.ShapeDtypeStruct((b, s, width), BF16),
        scratch_shapes=[pltpu.VMEM((hh, HEAD_DIM, HEAD_DIM), F32), buf, buf, buf, full, full, full,
                        gate_buf, gate_buf],
        compiler_params=_cparams(("parallel", "arbitrary")),
        name="gated_delta",
    )(lane_bcast(a_log), lane_bcast(dt_bias), proj3, proj3, proj3, proj3, gates4, gates4,
      conv_w, conv_w, conv_w, norm_w.reshape(1, HEAD_DIM))


def _rms_rows(x, w):
    ms = jnp.mean(x * x, axis=-1, keepdims=True)
    return x * lax.rsqrt(ms + EPS) * w


def _qproj_kernel(c_ref, nw_ref, w_ref, cos_ref, sin_ref, o_ref):
    hq = _rms_rows(c_ref[0].astype(F32), nw_ref[...]).astype(BF16)
    res = _dot(hq, w_ref[...])
    cos = cos_ref[0]
    sin = sin_ref[0]
    scale = SOFTMAX_SCALE * LOG2E
    for hd in range(MLA_HEADS):
        base = hd * QK_PAD
        nope = res[:, base:base + QK_NOPE]
        rp = res[:, base + QK_NOPE:base + QK_PAD]
        rot = rp * cos + _rotate_half_unsigned(rp) * sin
        o_ref[0, hd, :, 0:QK_NOPE] = (nope * scale).astype(BF16)
        o_ref[0, hd, :, QK_NOPE:QK_PAD] = (rot * scale).astype(BF16)


def _kvproj_kernel(c_ref, small_ref, nw_ref, w_ref, cos_ref, sin_ref, k_ref, v_ref):
    hkv = _rms_rows(c_ref[0].astype(F32), nw_ref[...]).astype(BF16)
    res = _dot(hkv, w_ref[...])
    cos = cos_ref[0]
    sin = sin_ref[0]
    kr = small_ref[0]
    kr16 = (kr * cos + _rotate_half_unsigned(kr) * sin).astype(BF16)
    nk = MLA_HEADS * QK_NOPE
    for hd in range(MLA_HEADS):
        k_ref[0, hd, :, 0:QK_NOPE] = res[:, hd * QK_NOPE:(hd + 1) * QK_NOPE].astype(BF16)
        k_ref[0, hd, :, QK_NOPE:QK_PAD] = kr16
        v_ref[0, hd] = res[:, nk + hd * V_HEAD:nk + (hd + 1) * V_HEAD].astype(BF16)


def _mla_projections(proj3, small3, q_norm_w, w_uq, kv_norm_w, w_ukv, cos_tab, sin_tab):
    b, s, _ = proj3.shape
    hh = MLA_HEADS
    tm = MLA_TM
    main_cq = (3 * GDN_HEADS * HEAD_DIM + GDN_HEADS * HEAD_DIM) // Q_LORA
    main_ckv = main_cq + 1
    wq = w_uq.reshape(Q_LORA, hh, QK_NOPE + QK_ROPE)
    gap = jnp.zeros((Q_LORA, hh, QK_PAD - QK_NOPE - QK_ROPE - QK_ROPE // 2), w_uq.dtype)
    wq = jnp.concatenate([wq[:, :, :QK_NOPE], _with_x2_duplicate(wq[:, :, QK_NOPE:], gap)], axis=2)
    wq = wq.reshape(Q_LORA, hh * QK_PAD).astype(BF16)
    wkv = w_ukv.reshape(KV_LORA, hh, QK_NOPE + V_HEAD)
    wkv = jnp.concatenate([wkv[:, :, :QK_NOPE].reshape(KV_LORA, hh * QK_NOPE),
                           wkv[:, :, QK_NOPE:].reshape(KV_LORA, hh * V_HEAD)], axis=1).astype(BF16)
    tab_spec = pl.BlockSpec((1, tm, LANES), lambda bi, ti: (bi, ti, 0))
    head_spec = lambda width: pl.BlockSpec((1, hh, tm, width), lambda bi, ti: (bi, 0, ti, 0))

    def qkv_kernel(cq_ref, ckv_ref, small_ref, qnw_ref, kvnw_ref, wq_ref, wkv_ref, cos_ref, sin_ref,
                   q_ref, k_ref, v_ref):
        _qproj_kernel(cq_ref, qnw_ref, wq_ref, cos_ref, sin_ref, q_ref)
        _kvproj_kernel(ckv_ref, small_ref, kvnw_ref, wkv_ref, cos_ref, sin_ref, k_ref, v_ref)

    return pl.pallas_call(
        qkv_kernel,
        grid=(b, s // tm),
        in_specs=[pl.BlockSpec((1, tm, Q_LORA), lambda bi, ti: (bi, ti, main_cq)),
                  pl.BlockSpec((1, tm, KV_LORA), lambda bi, ti: (bi, ti, main_ckv)),
                  tab_spec,
                  pl.BlockSpec((1, Q_LORA), lambda bi, ti: (0, 0)),
                  pl.BlockSpec((1, KV_LORA), lambda bi, ti: (0, 0)),
                  pl.BlockSpec((Q_LORA, hh * QK_PAD), lambda bi, ti: (0, 0)),
                  pl.BlockSpec((KV_LORA, hh * (QK_NOPE + V_HEAD)), lambda bi, ti: (0, 0)),
                  tab_spec, tab_spec],
        out_specs=[head_spec(QK_PAD), head_spec(QK_PAD), head_spec(V_HEAD)],
        out_shape=[jax.ShapeDtypeStruct((b, hh, s, QK_PAD), BF16),
                   jax.ShapeDtypeStruct((b, hh, s, QK_PAD), BF16),
                   jax.ShapeDtypeStruct((b, hh, s, V_HEAD), BF16)],
        compiler_params=_cparams(("parallel", "parallel")),
        name="mla_qkv_proj",
    )(proj3, proj3, small3, q_norm_w.reshape(1, Q_LORA), kv_norm_w.reshape(1, KV_LORA), wq, wkv,
      cos_tab, sin_tab)


def _attn_kernel(q_ref, k_ref, v_ref, nw_ref, o_ref, *scratch):
    nh = ATT_HEADS
    tk = ATT_TK
    streams = [(hd, half) for hd in range(nh) for half in range(2)]
    ns = len(streams)
    s_refs, (m_ref, acc_ref) = scratch[:ns], scratch[ns:]
    i = pl.program_id(2)
    kb = 2 * tk
    q = [q_ref[0, hd, half * tk:(half + 1) * tk, :] for hd, half in streams]
    m_ref[...] = jnp.full(m_ref.shape, -jnp.inf, F32)
    acc_ref[...] = jnp.zeros(acc_ref.shape, F32)

    def scores(st, start, n):
        return _dot_nt(q[st], k_ref[0, streams[st][0], pl.ds(start, n), :])

    def update(st, sc, start, n, visible_past=None):
        if visible_past is not None:
            qpos = lax.broadcasted_iota(jnp.int32, sc.shape, 0)
            kpos = lax.broadcasted_iota(jnp.int32, sc.shape, 1)
            sc = jnp.where(kpos <= qpos + visible_past, sc, -jnp.inf)
        vb = v_ref[0, streams[st][0], pl.ds(start, n), :]
        vb = jnp.concatenate([vb, jnp.ones((n, V_HEAD), BF16)], axis=1)
        m_prev = m_ref[st]
        m_new = jnp.maximum(m_prev, jnp.max(sc, axis=-1, keepdims=True))
        p = jnp.exp2(sc - jnp.concatenate([m_new] * (n // LANES), axis=1))
        alpha = jnp.exp2(m_prev - m_new)
        acc_ref[st] = jnp.concatenate([alpha, alpha], axis=1) * acc_ref[st] + _dot(p.astype(BF16), vb)
        m_ref[st] = m_new

    every = range(ns)

    def body(jj, carry):
        start = pl.multiple_of(jj * kb, kb)
        for st in every:
            s_refs[st][...] = scores(st, start, kb)
        for st in every:
            update(st, s_refs[st][...], start, kb)
        return carry

    lax.fori_loop(0, i, body, 0)

    start = pl.multiple_of(i * kb, kb)
    for st in every:
        if streams[st][1] == 0:
            s_refs[st][:, 0:tk] = scores(st, start, tk)
        else:
            s_refs[st][...] = scores(st, start, kb)
    for st in every:
        if streams[st][1] == 0:
            update(st, s_refs[st][:, 0:tk], start, tk, visible_past=0)
        else:
            update(st, s_refs[st][...], start, kb, visible_past=tk)

    nw = nw_ref[...]
    for st, (hd, half) in enumerate(streams):
        acc = acc_ref[st]
        o = acc[:, :V_HEAD] / acc[:, V_HEAD:]
        y = o * lax.rsqrt(jnp.mean(o * o, axis=-1, keepdims=True) + EPS) * nw
        o_ref[0, half * tk:(half + 1) * tk, hd * V_HEAD:(hd + 1) * V_HEAD] = y.astype(o_ref.dtype)


def _attention(q, k, v, norm_w):
    b, hh, s, _ = q.shape
    tk = ATT_TK
    tq = 2 * tk
    nh = ATT_HEADS
    ns = 2 * nh
    score_buf = pltpu.VMEM((tk, 2 * tk), F32)
    return pl.pallas_call(
        _attn_kernel,
        grid=(b, hh // nh, s // tq),
        in_specs=[pl.BlockSpec((1, nh, tq, QK_PAD), lambda bi, hi, qi: (bi, hi, qi, 0)),
                  pl.BlockSpec((1, nh, s, QK_PAD), lambda bi, hi, qi: (bi, hi, 0, 0)),
                  pl.BlockSpec((1, nh, s, V_HEAD), lambda bi, hi, qi: (bi, hi, 0, 0)),
                  pl.BlockSpec((1, V_HEAD), lambda bi, hi, qi: (0, 0))],
        out_specs=pl.BlockSpec((1, tq, nh * V_HEAD), lambda bi, hi, qi: (bi, qi, hi)),
        out_shape=jax.ShapeDtypeStruct((b, s, hh * V_HEAD), BF16),
        scratch_shapes=[score_buf] * ns + [pltpu.VMEM((ns, tk, V_HEAD), F32),
                                            pltpu.VMEM((ns, tk, 2 * V_HEAD), F32)],
        compiler_params=_cparams(("parallel", "parallel", "arbitrary")),
        name="mla_attention",
    )(q, k, v, norm_w.reshape(1, V_HEAD))


def _outproj_kernel(og_ref, om_ref, x_ref, wg_ref, wm_ref, o_ref):
    o_ref[...] = x_ref[...] + _dot(og_ref[...], wg_ref[...]) + _dot(om_ref[...], wm_ref[...])


def _out_projection(o_gdn, o_mla, x2, w_out):
    m, d = x2.shape
    kg = o_gdn.shape[1]
    km = o_mla.shape[1]
    tm = OUT_TM
    w16 = w_out.astype(BF16)
    return pl.pallas_call(
        _outproj_kernel,
        grid=(m // tm,),
        in_specs=[pl.BlockSpec((tm, kg), lambda i: (i, 0)),
                  pl.BlockSpec((tm, km), lambda i: (i, 0)),
                  pl.BlockSpec((tm, d), lambda i: (i, 0)),
                  pl.BlockSpec((kg, d), lambda i: (0, 0)),
                  pl.BlockSpec((km, d), lambda i: (0, 0))],
        out_specs=pl.BlockSpec((tm, d), lambda i: (i, 0)),
        out_shape=jax.ShapeDtypeStruct((m, d), F32),
        compiler_params=_cparams(("parallel",)),
        name="out_projection",
    )(o_gdn, o_mla, x2, w16[:kg], w16[kg:])


def _ffn_kernel(x_ref, nw_ref, wg_ref, wu_ref, wd_ref, fw_ref, o_ref, h_ref, *, nj):
    j = pl.program_id(1)

    tm = x_ref.shape[0]

    @pl.when(j == 0)
    def _():
        x = x_ref[...]
        h_ref[...] = _rms_rows(x, nw_ref[...]).astype(BF16)
        o_ref[...] = x

    halves = [slice(0, tm // 2), slice(tm // 2, tm)]
    wg = wg_ref[...]
    wu = wu_ref[...]
    gu = [(_dot(h_ref[rows, :], wg), _dot(h_ref[rows, :], wu)) for rows in halves]
    act = [(_silu(g) * u).astype(BF16) for g, u in gu]
    wd = wd_ref[...]
    for rows, a in zip(halves, act):
        o_ref[rows, :] += _dot(a, wd)

    @pl.when(j == nj - 1)
    def _():
        o_ref[...] = _rms_rows(o_ref[...], fw_ref[...])


def _ffn(x1, norm_w, w_gate, w_up, w_down, final_w):
    m, d = x1.shape
    dff = w_gate.shape[1]
    tm = FFN_TM
    tf = FFN_TF
    nj = dff // tf
    kern = functools.partial(_ffn_kernel, nj=nj)
    return pl.pallas_call(
        kern,
        grid=(m // tm, nj),
        in_specs=[pl.BlockSpec((tm, d), lambda i, j: (i, 0)),
                  pl.BlockSpec((1, d), lambda i, j: (0, 0)),
                  pl.BlockSpec((d, tf), lambda i, j: (0, j)),
                  pl.BlockSpec((d, tf), lambda i, j: (0, j)),
                  pl.BlockSpec((tf, d), lambda i, j: (j, 0)),
                  pl.BlockSpec((1, d), lambda i, j: (0, 0))],
        out_specs=pl.BlockSpec((tm, d), lambda i, j: (i, 0)),
        out_shape=jax.ShapeDtypeStruct((m, d), F32),
        scratch_shapes=[pltpu.VMEM((tm, d), BF16)],
        compiler_params=_cparams(("parallel", "arbitrary")),
        name="swiglu_ffn",
    )(x1, norm_w.reshape(1, d), w_gate.astype(BF16), w_up.astype(BF16), w_down.astype(BF16),
      final_w.reshape(1, d))


def kernel(x, positions, attn_norm_w, w_in, conv_w, a_log, dt_bias, gdn_norm_w, q_norm_w, w_uq,
           kv_norm_w, w_ukv, mla_out_norm_w, w_out, ffn_norm_w, w_gate, w_up, w_down, final_norm_w):
    b, s, d = x.shape
    assert w_in.shape[0] == 1, "the final rmsnorm is fused into the (single) layer's FFN kernel"
    l = 0
    cos_tab, sin_tab = _rope_tables(positions)
    x2 = x.reshape(b * s, d)
    proj, small, gates = _in_projection(x2, attn_norm_w[l], w_in[l])
    proj3 = proj.reshape(b, s, proj.shape[1])
    small3 = small.reshape(b, s, LANES)
    o_gdn = _gated_delta(proj3, gates, conv_w[l], a_log[l], dt_bias[l], gdn_norm_w[l])
    q, k, v = _mla_projections(proj3, small3, q_norm_w[l], w_uq[l], kv_norm_w[l], w_ukv[l],
                               cos_tab, sin_tab)
    o_mla = _attention(q, k, v, mla_out_norm_w[l])
    x1 = _out_projection(o_gdn.reshape(b * s, -1), o_mla.reshape(b * s, -1), x2, w_out[l])
    out = _ffn(x1, ffn_norm_w[l], w_gate[l], w_up[l], w_down[l], final_norm_w)
    return out.reshape(b, s, d)
```

```python
import functools
import math

import jax
import jax.numpy as jnp
from jax import lax
from jax.experimental import pallas as pl
from jax.experimental.pallas import tpu as pltpu

F32 = jnp.float32
BF16 = jnp.bfloat16

GDN_HEADS = 8
HEAD_DIM = 128
GDN_CONV = 4
GDN_CHUNK = 64
MLA_HEADS = 8
QK_NOPE = 128
QK_ROPE = 64
V_HEAD = 128
Q_LORA = 512
KV_LORA = 512
ROPE_THETA = 10000.0
EPS = 1e-6

LANES = 128
VMEM_LIMIT_BYTES = 56 * 2**20

INPROJ_TM = 1024
INPROJ_TN = 1792
GDN_T = 512
GDN_BLK = 2 * GDN_CHUNK
GDN_UNROLL = 2
MLA_TM = 1024
ATT_TK = 512
ATT_HEADS = 2
OUT_TM = 512
FFN_TM = 1024
FFN_TF = 512

QK_PAD = 256
SOFTMAX_SCALE = (QK_NOPE + QK_ROPE) ** -0.5
LOG2E = math.log2(math.e)


def _cparams(semantics):
    return pltpu.CompilerParams(dimension_semantics=semantics, vmem_limit_bytes=VMEM_LIMIT_BYTES)


def _sigmoid(x):
    return 1.0 / (1.0 + jnp.exp(-x))


def _silu(x):
    h = 0.5 * x
    return h + h * jnp.tanh(h)


def _softplus(x):
    return jnp.maximum(x, 0.0) + jnp.log1p(jnp.exp(-jnp.abs(x)))


def _dot(a, b):
    return jnp.dot(a, b, preferred_element_type=F32)


def _dot_nt(a, b):
    return lax.dot_general(a, b, (((1,), (1,)), ((), ())), preferred_element_type=F32)


def _dot_tn(a, b):
    return lax.dot_general(a, b, (((0,), (0,)), ((), ())), preferred_element_type=F32)


def _rope_kernel(pos_ref, invf_ref, cos_ref, sin_ref):
    half = QK_ROPE // 2
    per_row = LANES // half
    rows = pos_ref.shape[0]
    ang = pos_ref[...].astype(F32) * invf_ref[...]
    cos = jnp.cos(ang)
    sin = jnp.sin(ang)
    lane = lax.broadcasted_iota(jnp.int32, ang.shape, 1)
    for g in range(per_row):
        to_front = lambda x: pltpu.roll(x, LANES - g * half, axis=1) if g else x
        c1 = to_front(cos)
        s1 = to_front(sin)
        c2 = pltpu.roll(c1, half, axis=1)
        s2 = pltpu.roll(s1, half, axis=1)
        token_rows = pl.ds(g, rows, stride=per_row)
        cos_ref[token_rows, :] = jnp.where(lane < half, c1, jnp.where(lane < QK_ROPE, c2, 0.0))
        sin_ref[token_rows, :] = jnp.where(lane < half, -s1, jnp.where(lane < QK_ROPE, s2, 0.0))


def _rope_inputs(positions):
    b, s = positions.shape
    half = QK_ROPE // 2
    per_row = LANES // half
    rows = b * s // per_row
    inv_freq = ROPE_THETA ** (-jnp.arange(half, dtype=F32) / half)
    invf = jnp.tile(inv_freq, per_row).reshape(1, LANES)
    pos_rep = jnp.broadcast_to(positions.reshape(rows, per_row, 1), (rows, per_row, half)).reshape(rows, LANES)
    return pos_rep, invf


def _rotate_half_unsigned(x):
    return pltpu.roll(x, QK_ROPE // 2, axis=1)


def _with_x2_duplicate(w_rope, gap):
    return jnp.concatenate([w_rope, gap, w_rope[..., QK_ROPE // 2:]], axis=-1)


def _inproj_kernel(x_ref, nw_ref, w_ref, o_ref, small_ref, gates_ref, h_ref, *, nj, small_off):
    j = pl.program_id(1)

    @pl.when(j == 0)
    def _():
        x = x_ref[...]
        ms = jnp.mean(x * x, axis=-1, keepdims=True)
        h_ref[...] = (x * lax.rsqrt(ms + EPS) * nw_ref[...]).astype(BF16)

    acc = _dot(h_ref[...], w_ref[...])
    o_ref[...] = acc.astype(BF16)

    @pl.when(j == nj - 1)
    def _():
        small = acc[:, small_off:small_off + LANES]
        small_ref[...] = small
        for r in range(gates_ref.shape[0]):
            tile = small[r * LANES:(r + 1) * LANES, :].T
            gates_ref[r] = tile[QK_ROPE:QK_ROPE + 2 * GDN_HEADS, :]


def _in_projection(x2, norm_w, w_in):
    m, d = x2.shape
    conv_ch = 3 * GDN_HEADS * HEAD_DIM
    gdn_v = GDN_HEADS * HEAD_DIM
    w_in = w_in.astype(BF16)
    o = 0
    qkv = w_in[:, o:o + conv_ch]; o += conv_ch
    z = w_in[:, o:o + gdn_v]; o += gdn_v
    b_raw = w_in[:, o:o + GDN_HEADS]; o += GDN_HEADS
    a_raw = w_in[:, o:o + GDN_HEADS]; o += GDN_HEADS
    cq = w_in[:, o:o + Q_LORA]; o += Q_LORA
    ckv = w_in[:, o:o + KV_LORA]; o += KV_LORA
    kr = w_in[:, o:o + QK_ROPE]
    main_w = conv_ch + gdn_v + Q_LORA + KV_LORA
    tn = INPROJ_TN
    total = pl.cdiv(main_w + LANES, tn) * tn
    gates_w = jnp.concatenate([b_raw, a_raw], axis=1)
    gap = jnp.zeros((d, LANES - QK_ROPE - 2 * GDN_HEADS - QK_ROPE // 2), w_in.dtype)
    small_w = _with_x2_duplicate(kr, jnp.concatenate([gates_w, gap], axis=1))
    pad = jnp.zeros((d, total - main_w - LANES), w_in.dtype)
    w = jnp.concatenate([qkv, z, cq, ckv, small_w, pad], axis=1)
    nj = total // tn
    small_off = main_w - (nj - 1) * tn
    tm = INPROJ_TM
    kern = functools.partial(_inproj_kernel, nj=nj, small_off=small_off)
    return pl.pallas_call(
        kern,
        grid=(m // tm, nj),
        in_specs=[pl.BlockSpec((tm, d), lambda i, j: (i, 0)),
                  pl.BlockSpec((1, d), lambda i, j: (0, 0)),
                  pl.BlockSpec((d, tn), lambda i, j: (0, j))],
        out_specs=[pl.BlockSpec((tm, tn), lambda i, j: (i, j)),
                   pl.BlockSpec((tm, LANES), lambda i, j: (i, 0)),
                   pl.BlockSpec((tm // LANES, 2 * GDN_HEADS, LANES), lambda i, j: (i, 0, 0))],
        out_shape=[jax.ShapeDtypeStruct((m, total), BF16),
                   jax.ShapeDtypeStruct((m, LANES), F32),
                   jax.ShapeDtypeStruct((m // LANES, 2 * GDN_HEADS, LANES), F32)],
        scratch_shapes=[pltpu.VMEM((tm, d), BF16)],
        compiler_params=_cparams(("parallel", "arbitrary")),
        name="in_projection",
    )(x2, norm_w.reshape(1, d), w)


def _gdn_kernel(alog_ref, dtb_ref, q_ref, k_ref, v_ref, z_ref, a_ref, b_ref,
                cwq_ref, cwk_ref, cwv_ref, nw_ref, o_ref,
                state_ref, xq_ref, xk_ref, xv_ref, qn_ref, kn_ref, vv_ref, gc_ref, beta_ref):
    t = pl.program_id(1)
    T = q_ref.shape[1]
    nh = GDN_HEADS
    blk = GDN_BLK
    hist = blk
    c = GDN_CHUNK

    @pl.when(t == 0)
    def _():
        state_ref[...] = jnp.zeros_like(state_ref)
        zero_hist = jnp.zeros((hist, nh * HEAD_DIM), BF16)
        xq_ref[0:hist, :] = zero_hist
        xk_ref[0:hist, :] = zero_hist
        xv_ref[0:hist, :] = zero_hist

    taps = GDN_CONV - 1
    sel_row = lax.broadcasted_iota(jnp.int32, (taps * blk, hist + blk), 0)
    sel_col = lax.broadcasted_iota(jnp.int32, (taps * blk, hist + blk), 1)
    shift_sel = jnp.where(sel_col == sel_row % blk + hist - taps + sel_row // blk, 1.0, 0.0).astype(BF16)

    def conv_silu(xbuf, cw_ref, m):
        window = xbuf[m * blk:m * blk + hist + blk, :]
        w = cw_ref[...]
        shifted = _dot(shift_sel, window)
        y = window[hist:, :].astype(F32) * w[taps:taps + 1, :]
        for i in range(taps):
            y = y + shifted[i * blk:(i + 1) * blk, :] * w[i:i + 1, :]
        return _silu(y)

    for in_ref, xbuf in ((q_ref, xq_ref), (k_ref, xk_ref), (v_ref, xv_ref)):
        xbuf[hist:hist + T, :] = in_ref[0]
    for m in range(T // blk):
        rows = slice(m * blk, (m + 1) * blk)
        q = conv_silu(xq_ref, cwq_ref, m)
        k = conv_silu(xk_ref, cwk_ref, m)
        vv_ref[rows, :] = conv_silu(xv_ref, cwv_ref, m)
        for hd in range(nh):
            sl = slice(hd * HEAD_DIM, (hd + 1) * HEAD_DIM)
            qh = q[:, sl]
            kh = k[:, sl]
            qn_ref[rows, sl] = qh * (lax.rsqrt(jnp.sum(qh * qh, axis=-1, keepdims=True) + EPS) * (HEAD_DIM ** -0.5))
            kn_ref[rows, sl] = kh * lax.rsqrt(jnp.sum(kh * kh, axis=-1, keepdims=True) + EPS)
    for xbuf in (xq_ref, xk_ref, xv_ref):
        xbuf[0:hist, :] = xbuf[T:T + hist, :]

    neg_a = -jnp.exp(alog_ref[...])
    dtb = dtb_ref[...]
    lane = lax.broadcasted_iota(jnp.int32, (nh, LANES), 1)
    in_chunk = lane % c
    for r in range(T // LANES):
        tile = pl.ds(r * nh, nh)
        beta_ref[tile, :] = _sigmoid(b_ref[0, r])
        gc = neg_a * _softplus(a_ref[0, r] + dtb)
        s = 1
        while s < c:
            gc = gc + jnp.where(in_chunk >= s, pltpu.roll(gc, s, axis=1), 0.0)
            s *= 2
        gc_ref[tile, :] = gc

    row = lax.broadcasted_iota(jnp.int32, (blk, blk), 0)
    col = lax.broadcasted_iota(jnp.int32, (blk, blk), 1)
    same_chunk = (row // c) == (col // c)
    m_tril = jnp.logical_and(same_chunk, row >= col)
    m_strict = jnp.logical_and(same_chunk, row > col)
    eye = jnp.where(row == col, 1.0, 0.0).astype(F32)
    first = row < c

    heads = range(nh)
    sls = [slice(hd * HEAD_DIM, (hd + 1) * HEAD_DIM) for hd in heads]
    bf = lambda xs: [x.astype(BF16) for x in xs]

    unroll = GDN_UNROLL
    streams = [(sb, hd) for sb in range(unroll) for hd in heads]

    def block(rr, carry):
        rows = [pl.ds(pl.multiple_of((rr * unroll + sb) * blk, blk), blk) for sb in range(unroll)]
        gate = lambda sb, hd: pl.ds((rr * unroll + sb) * nh + hd, 1)
        qn = [qn_ref[rows[sb], sls[hd]] for sb, hd in streams]
        kn = [kn_ref[rows[sb], sls[hd]] for sb, hd in streams]
        v = [vv_ref[rows[sb], sls[hd]] for sb, hd in streams]
        g_row = [jnp.broadcast_to(gc_ref[gate(sb, hd), :], (blk, blk)) for sb, hd in streams]
        g_col = [x.T for x in g_row]
        b_col = [jnp.broadcast_to(beta_ref[gate(sb, hd), :], (blk, blk)).T for sb, hd in streams]
        decay = [jnp.exp(jnp.where(m_tril, gc_ - gr_, -jnp.inf)) for gc_, gr_ in zip(g_col, g_row)]
        kb = [a * b for a, b in zip(kn, b_col)]
        vb = [a * b for a, b in zip(v, b_col)]
        kn16 = bf(kn)
        kk = [_dot_nt(a, b) for a, b in zip(bf(kb), kn16)]
        qk = [_dot_nt(a, b) for a, b in zip(bf(qn), kn16)]
        a_mat = [jnp.where(m_strict, x * d, 0.0) for x, d in zip(kk, decay)]
        intra16 = bf([x * d for x, d in zip(qk, decay)])
        inv = [eye - a for a in a_mat]
        p16 = bf(a_mat)
        n = 2
        while n < c:
            p16 = bf([_dot(x, x) for x in p16])
            inv = [i + _dot(i16, x) for i, i16, x in zip(inv, bf(inv), p16)]
            n *= 2
        eg = [jnp.exp(x) for x in g_col]
        rhs = bf([jnp.concatenate([a, b * e], axis=1) for a, b, e in zip(vb, kb, eg)])
        uw = [_dot(i16, x) for i16, x in zip(bf(inv), rhs)]
        u = [x[:, :HEAD_DIM] for x in uw]
        w16 = bf([x[:, HEAD_DIM:] for x in uw])
        g_last1 = [jnp.broadcast_to(x[c - 1:c, :], (blk, blk)) for x in g_col]
        g_last2 = [jnp.broadcast_to(x[blk - 1:blk, :], (blk, blk)) for x in g_col]
        kdec16 = bf([k_ * jnp.exp(jnp.where(first, l1, l2) - gc_)
                     for k_, l1, l2, gc_ in zip(kn, g_last1, g_last2, g_col)])
        qg16 = bf([a * e for a, e in zip(qn, eg)])
        e1 = [jnp.exp(x) for x in g_last1]
        e2 = [jnp.exp(x) for x in g_last2]

        state = [state_ref[hd] for hd in heads]
        for sb in range(unroll):
            mine = lambda xs: xs[sb * nh:(sb + 1) * nh]
            u_, w_, q_, k_ = mine(u), mine(w16), mine(qg16), mine(kdec16)
            s016 = bf(state)
            vn1 = [a[0:c] - _dot(b[0:c], s_) for a, b, s_ in zip(u_, w_, s016)]
            oq1 = [_dot(a[0:c], s_) for a, s_ in zip(q_, s016)]
            s1 = [s_ * e_ + _dot_tn(a[0:c], x) for s_, e_, a, x in zip(state, mine(e1), k_, bf(vn1))]
            s116 = bf(s1)
            vn2 = [a[c:blk] - _dot(b[c:blk], s_) for a, b, s_ in zip(u_, w_, s116)]
            oq2 = [_dot(a[c:blk], s_) for a, s_ in zip(q_, s116)]
            state = [s_ * e_ + _dot_tn(a[c:blk], x) for s_, e_, a, x in zip(s1, mine(e2), k_, bf(vn2))]
            for hd in heads:
                vn = jnp.concatenate([vn1[hd], vn2[hd]], axis=0).astype(BF16)
                o = jnp.concatenate([oq1[hd], oq2[hd]], axis=0) + _dot(mine(intra16)[hd], vn)
                zz = z_ref[0, rows[sb], sls[hd]].astype(F32)
                y = o * lax.rsqrt(jnp.mean(o * o, axis=-1, keepdims=True) + EPS) * nw
                o_ref[0, rows[sb], sls[hd]] = (y * _silu(zz)).astype(o_ref.dtype)
        for hd in heads:
            state_ref[hd] = state[hd]
        return carry

    nw = nw_ref[...]
    lax.fori_loop(0, T // (blk * unroll), block, 0)


def _gated_delta(proj3, gates, conv_w, a_log, dt_bias, norm_w):
    b, s, _ = proj3.shape
    hh = GDN_HEADS
    T = GDN_T
    tiles = T // LANES
    width = hh * HEAD_DIM
    gates4 = gates.reshape(b, s // LANES, 2 * hh, LANES)
    lane_bcast = lambda p: jnp.broadcast_to(p.reshape(hh, 1), (hh, LANES))
    col_spec = lambda j: pl.BlockSpec((1, T, width), lambda bi, ti: (bi, ti, j))
    gate_spec = lambda j: pl.BlockSpec((1, tiles, hh, LANES), lambda bi, ti: (bi, ti, j, 0))
    cw_spec = lambda j: pl.BlockSpec((GDN_CONV, width), lambda bi, ti: (0, j))
    head_spec = pl.BlockSpec((hh, LANES), lambda bi, ti: (0, 0))
    buf = pltpu.VMEM((T + GDN_BLK, width), BF16)
    full = pltpu.VMEM((T, width), F32)
    gate_buf = pltpu.VMEM((tiles * hh, LANES), F32)
    return pl.pallas_call(
        _gdn_kernel,
        grid=(b, s // T),
        in_specs=[head_spec, head_spec,
                  col_spec(0), col_spec(1), col_spec(2), col_spec(3),
                  gate_spec(1), gate_spec(0),
                  cw_spec(0), cw_spec(1), cw_spec(2),
                  pl.BlockSpec((1, HEAD_DIM), lambda bi, ti: (0, 0))],
        out_specs=pl.BlockSpec((1, T, width), lambda bi, ti: (bi, ti, 0)),
        out_shape=jax.ShapeDtypeStruct((b, s, width), BF16),
        scratch_shapes=[pltpu.VMEM((hh, HEAD_DIM, HEAD_DIM), F32), buf, buf, buf, full, full, full,
                        gate_buf, gate_buf],
        compiler_params=_cparams(("parallel", "arbitrary")),
        name="gated_delta",
    )(lane_bcast(a_log), lane_bcast(dt_bias), proj3, proj3, proj3, proj3, gates4, gates4,
      conv_w, conv_w, conv_w, norm_w.reshape(1, HEAD_DIM))


def _rms_rows(x, w):
    ms = jnp.mean(x * x, axis=-1, keepdims=True)
    return x * lax.rsqrt(ms + EPS) * w


def _qproj_kernel(c_ref, nw_ref, w_ref, cos_ref, sin_ref, o_ref):
    hq = _rms_rows(c_ref[0].astype(F32), nw_ref[...]).astype(BF16)
    res = _dot(hq, w_ref[...])
    cos = cos_ref[0]
    sin = sin_ref[0]
    scale = SOFTMAX_SCALE * LOG2E
    for hd in range(MLA_HEADS):
        base = hd * QK_PAD
        nope = res[:, base:base + QK_NOPE]
        rp = res[:, base + QK_NOPE:base + QK_PAD]
        rot = rp * cos + _rotate_half_unsigned(rp) * sin
        o_ref[0, hd, :, 0:QK_NOPE] = (nope * scale).astype(BF16)
        o_ref[0, hd, :, QK_NOPE:QK_PAD] = (rot * scale).astype(BF16)


def _kvproj_kernel(c_ref, small_ref, nw_ref, w_ref, cos_ref, sin_ref, k_ref, v_ref):
    hkv = _rms_rows(c_ref[0].astype(F32), nw_ref[...]).astype(BF16)
    res = _dot(hkv, w_ref[...])
    cos = cos_ref[0]
    sin = sin_ref[0]
    kr = small_ref[0]
    kr16 = (kr * cos + _rotate_half_unsigned(kr) * sin).astype(BF16)
    nk = MLA_HEADS * QK_NOPE
    for hd in range(MLA_HEADS):
        k_ref[0, hd, :, 0:QK_NOPE] = res[:, hd * QK_NOPE:(hd + 1) * QK_NOPE].astype(BF16)
        k_ref[0, hd, :, QK_NOPE:QK_PAD] = kr16
        v_ref[0, hd] = res[:, nk + hd * V_HEAD:nk + (hd + 1) * V_HEAD].astype(BF16)


def _mla_projections(proj3, small3, q_norm_w, w_uq, kv_norm_w, w_ukv, positions):
    b, s, _ = proj3.shape
    hh = MLA_HEADS
    tm = MLA_TM
    main_cq = (3 * GDN_HEADS * HEAD_DIM + GDN_HEADS * HEAD_DIM) // Q_LORA
    main_ckv = main_cq + 1
    wq = w_uq.reshape(Q_LORA, hh, QK_NOPE + QK_ROPE)
    gap = jnp.zeros((Q_LORA, hh, QK_PAD - QK_NOPE - QK_ROPE - QK_ROPE // 2), w_uq.dtype)
    wq = jnp.concatenate([wq[:, :, :QK_NOPE], _with_x2_duplicate(wq[:, :, QK_NOPE:], gap)], axis=2)
    wq = wq.reshape(Q_LORA, hh * QK_PAD).astype(BF16)
    wkv = w_ukv.reshape(KV_LORA, hh, QK_NOPE + V_HEAD)
    wkv = jnp.concatenate([wkv[:, :, :QK_NOPE].reshape(KV_LORA, hh * QK_NOPE),
                           wkv[:, :, QK_NOPE:].reshape(KV_LORA, hh * V_HEAD)], axis=1).astype(BF16)
    tab_spec = pl.BlockSpec((1, tm, LANES), lambda bi, ti: (bi, ti, 0))
    head_spec = lambda width: pl.BlockSpec((1, hh, tm, width), lambda bi, ti: (bi, 0, ti, 0))

    per_row = LANES // (QK_ROPE // 2)
    pos_rep, invf = _rope_inputs(positions)
    steps = s // tm

    def qkv_kernel(cq_ref, ckv_ref, small_ref, qnw_ref, kvnw_ref, wq_ref, wkv_ref, pos_ref, invf_ref,
                   q_ref, k_ref, v_ref, cos_ref, sin_ref):
        _rope_kernel(pos_ref, invf_ref, cos_ref.at[0], sin_ref.at[0])
        _qproj_kernel(cq_ref, qnw_ref, wq_ref, cos_ref, sin_ref, q_ref)
        _kvproj_kernel(ckv_ref, small_ref, kvnw_ref, wkv_ref, cos_ref, sin_ref, k_ref, v_ref)

    return pl.pallas_call(
        qkv_kernel,
        grid=(b, steps),
        in_specs=[pl.BlockSpec((1, tm, Q_LORA), lambda bi, ti: (bi, ti, main_cq)),
                  pl.BlockSpec((1, tm, KV_LORA), lambda bi, ti: (bi, ti, main_ckv)),
                  tab_spec,
                  pl.BlockSpec((1, Q_LORA), lambda bi, ti: (0, 0)),
                  pl.BlockSpec((1, KV_LORA), lambda bi, ti: (0, 0)),
                  pl.BlockSpec((Q_LORA, hh * QK_PAD), lambda bi, ti: (0, 0)),
                  pl.BlockSpec((KV_LORA, hh * (QK_NOPE + V_HEAD)), lambda bi, ti: (0, 0)),
                  pl.BlockSpec((tm // per_row, LANES), lambda bi, ti: (bi * steps + ti, 0)),
                  pl.BlockSpec((1, LANES), lambda bi, ti: (0, 0))],
        out_specs=[head_spec(QK_PAD), head_spec(QK_PAD), head_spec(V_HEAD)],
        out_shape=[jax.ShapeDtypeStruct((b, hh, s, QK_PAD), BF16),
                   jax.ShapeDtypeStruct((b, hh, s, QK_PAD), BF16),
                   jax.ShapeDtypeStruct((b, hh, s, V_HEAD), BF16)],
        scratch_shapes=[pltpu.VMEM((1, tm, LANES), F32), pltpu.VMEM((1, tm, LANES), F32)],
        compiler_params=_cparams(("parallel", "parallel")),
        name="mla_qkv_proj",
    )(proj3, proj3, small3, q_norm_w.reshape(1, Q_LORA), kv_norm_w.reshape(1, KV_LORA), wq, wkv,
      pos_rep, invf)


def _attn_kernel(q_ref, k_ref, v_ref, nw_ref, o_ref, *scratch):
    nh = ATT_HEADS
    tk = ATT_TK
    streams = [(hd, half) for hd in range(nh) for half in range(2)]
    ns = len(streams)
    s_refs, (m_ref, acc_ref) = scratch[:ns], scratch[ns:]
    i = pl.program_id(2)
    kb = 2 * tk
    q = [q_ref[0, hd, half * tk:(half + 1) * tk, :] for hd, half in streams]
    m_ref[...] = jnp.full(m_ref.shape, -jnp.inf, F32)
    acc_ref[...] = jnp.zeros(acc_ref.shape, F32)

    def scores(st, start, n):
        return _dot_nt(q[st], k_ref[0, streams[st][0], pl.ds(start, n), :])

    def update(st, sc, start, n, visible_past=None):
        if visible_past is not None:
            qpos = lax.broadcasted_iota(jnp.int32, sc.shape, 0)
            kpos = lax.broadcasted_iota(jnp.int32, sc.shape, 1)
            sc = jnp.where(kpos <= qpos + visible_past, sc, -jnp.inf)
        vb = v_ref[0, streams[st][0], pl.ds(start, n), :]
        vb = jnp.concatenate([vb, jnp.ones((n, V_HEAD), BF16)], axis=1)
        m_prev = m_ref[st]
        m_new = jnp.maximum(m_prev, jnp.max(sc, axis=-1, keepdims=True))
        p = jnp.exp2(sc - jnp.concatenate([m_new] * (n // LANES), axis=1))
        alpha = jnp.exp2(m_prev - m_new)
        acc_ref[st] = jnp.concatenate([alpha, alpha], axis=1) * acc_ref[st] + _dot(p.astype(BF16), vb)
        m_ref[st] = m_new

    every = range(ns)

    def body(jj, carry):
        start = pl.multiple_of(jj * kb, kb)
        for st in every:
            s_refs[st][...] = scores(st, start, kb)
        for st in every:
            update(st, s_refs[st][...], start, kb)
        return carry

    lax.fori_loop(0, i, body, 0)

    start = pl.multiple_of(i * kb, kb)
    for st in every:
        if streams[st][1] == 0:
            s_refs[st][:, 0:tk] = scores(st, start, tk)
        else:
            s_refs[st][...] = scores(st, start, kb)
    for st in every:
        if streams[st][1] == 0:
            update(st, s_refs[st][:, 0:tk], start, tk, visible_past=0)
        else:
            update(st, s_refs[st][...], start, kb, visible_past=tk)

    nw = nw_ref[...]
    for st, (hd, half) in enumerate(streams):
        acc = acc_ref[st]
        o = acc[:, :V_HEAD] / acc[:, V_HEAD:]
        y = o * lax.rsqrt(jnp.mean(o * o, axis=-1, keepdims=True) + EPS) * nw
        o_ref[0, half * tk:(half + 1) * tk, hd * V_HEAD:(hd + 1) * V_HEAD] = y.astype(o_ref.dtype)


def _attention(q, k, v, norm_w):
    b, hh, s, _ = q.shape
    tk = ATT_TK
    tq = 2 * tk
    nh = ATT_HEADS
    ns = 2 * nh
    score_buf = pltpu.VMEM((tk, 2 * tk), F32)
    return pl.pallas_call(
        _attn_kernel,
        grid=(b, hh // nh, s // tq),
        in_specs=[pl.BlockSpec((1, nh, tq, QK_PAD), lambda bi, hi, qi: (bi, hi, qi, 0)),
                  pl.BlockSpec((1, nh, s, QK_PAD), lambda bi, hi, qi: (bi, hi, 0, 0)),
                  pl.BlockSpec((1, nh, s, V_HEAD), lambda bi, hi, qi: (bi, hi, 0, 0)),
                  pl.BlockSpec((1, V_HEAD), lambda bi, hi, qi: (0, 0))],
        out_specs=pl.BlockSpec((1, tq, nh * V_HEAD), lambda bi, hi, qi: (bi, qi, hi)),
        out_shape=jax.ShapeDtypeStruct((b, s, hh * V_HEAD), BF16),
        scratch_shapes=[score_buf] * ns + [pltpu.VMEM((ns, tk, V_HEAD), F32),
                                            pltpu.VMEM((ns, tk, 2 * V_HEAD), F32)],
        compiler_params=_cparams(("parallel", "parallel", "arbitrary")),
        name="mla_attention",
    )(q, k, v, norm_w.reshape(1, V_HEAD))


def _outproj_kernel(og_ref, om_ref, x_ref, wg_ref, wm_ref, o_ref):
    o_ref[...] = x_ref[...] + _dot(og_ref[...], wg_ref[...]) + _dot(om_ref[...], wm_ref[...])


def _out_projection(o_gdn, o_mla, x2, w_out):
    m, d = x2.shape
    kg = o_gdn.shape[1]
    km = o_mla.shape[1]
    tm = OUT_TM
    w16 = w_out.astype(BF16)
    return pl.pallas_call(
        _outproj_kernel,
        grid=(m // tm,),
        in_specs=[pl.BlockSpec((tm, kg), lambda i: (i, 0)),
                  pl.BlockSpec((tm, km), lambda i: (i, 0)),
                  pl.BlockSpec((tm, d), lambda i: (i, 0)),
                  pl.BlockSpec((kg, d), lambda i: (0, 0)),
                  pl.BlockSpec((km, d), lambda i: (0, 0))],
        out_specs=pl.BlockSpec((tm, d), lambda i: (i, 0)),
        out_shape=jax.ShapeDtypeStruct((m, d), F32),
        compiler_params=_cparams(("parallel",)),
        name="out_projection",
    )(o_gdn, o_mla, x2, w16[:kg], w16[kg:])


def _ffn_kernel(x_ref, nw_ref, wg_ref, wu_ref, wd_ref, fw_ref, o_ref, h_ref, *, nj):
    j = pl.program_id(1)

    tm = x_ref.shape[0]

    @pl.when(j == 0)
    def _():
        x = x_ref[...]
        h_ref[...] = _rms_rows(x, nw_ref[...]).astype(BF16)
        o_ref[...] = x

    halves = [slice(0, tm // 2), slice(tm // 2, tm)]
    wg = wg_ref[...]
    wu = wu_ref[...]
    gu = [(_dot(h_ref[rows, :], wg), _dot(h_ref[rows, :], wu)) for rows in halves]
    act = [(_silu(g) * u).astype(BF16) for g, u in gu]
    wd = wd_ref[...]
    for rows, a in zip(halves, act):
        o_ref[rows, :] += _dot(a, wd)

    @pl.when(j == nj - 1)
    def _():
        o_ref[...] = _rms_rows(o_ref[...], fw_ref[...])


def _ffn(x1, norm_w, w_gate, w_up, w_down, final_w):
    m, d = x1.shape
    dff = w_gate.shape[1]
    tm = FFN_TM
    tf = FFN_TF
    nj = dff // tf
    kern = functools.partial(_ffn_kernel, nj=nj)
    return pl.pallas_call(
        kern,
        grid=(m // tm, nj),
        in_specs=[pl.BlockSpec((tm, d), lambda i, j: (i, 0)),
                  pl.BlockSpec((1, d), lambda i, j: (0, 0)),
                  pl.BlockSpec((d, tf), lambda i, j: (0, j)),
                  pl.BlockSpec((d, tf), lambda i, j: (0, j)),
                  pl.BlockSpec((tf, d), lambda i, j: (j, 0)),
                  pl.BlockSpec((1, d), lambda i, j: (0, 0))],
        out_specs=pl.BlockSpec((tm, d), lambda i, j: (i, 0)),
        out_shape=jax.ShapeDtypeStruct((m, d), F32),
        scratch_shapes=[pltpu.VMEM((tm, d), BF16)],
        compiler_params=_cparams(("parallel", "arbitrary")),
        name="swiglu_ffn",
    )(x1, norm_w.reshape(1, d), w_gate.astype(BF16), w_up.astype(BF16), w_down.astype(BF16),
      final_w.reshape(1, d))


def kernel(x, positions, attn_norm_w, w_in, conv_w, a_log, dt_bias, gdn_norm_w, q_norm_w, w_uq,
           kv_norm_w, w_ukv, mla_out_norm_w, w_out, ffn_norm_w, w_gate, w_up, w_down, final_norm_w):
    b, s, d = x.shape
    assert w_in.shape[0] == 1, "the final rmsnorm is fused into the (single) layer's FFN kernel"
    l = 0
    x2 = x.reshape(b * s, d)
    proj, small, gates = _in_projection(x2, attn_norm_w[l], w_in[l])
    proj3 = proj.reshape(b, s, proj.shape[1])
    small3 = small.reshape(b, s, LANES)
    o_gdn = _gated_delta(proj3, gates, conv_w[l], a_log[l], dt_bias[l], gdn_norm_w[l])
    q, k, v = _mla_projections(proj3, small3, q_norm_w[l], w_uq[l], kv_norm_w[l], w_ukv[l],
                               positions)
    o_mla = _attention(q, k, v, mla_out_norm_w[l])
    x1 = _out_projection(o_gdn.reshape(b * s, -1), o_mla.reshape(b * s, -1), x2, w_out[l])
    out = _ffn(x1, ffn_norm_w[l], w_gate[l], w_up[l], w_down[l], final_norm_w)
    return out.reshape(b, s, d)
```
